```python
import math
import jax, jax.numpy as jnp
from jax import lax
import numpy as np

D_MODEL = 1024
BATCH = 1
SEQ = 16384
DEPTH = 4

GRID_W = 64
CTX_LEN = 256
EPS = 1e-6
N_MOD = 6
D_MIX = D_MODEL
D_GROUP = D_MIX // 4
HY_WIDTH = D_GROUP
HY_ORDER = 2
HY_SHORT = 3
HY_FREQS = 16
HY_EMB = 1 + 2 * HY_FREQS
HY_FFN = 64
HY_SIN_FREQ = 1.0
HY_DECAY_MIN = 3.07
HY_DECAY_MAX = 15.35
S5_WIDTH = D_GROUP
S5_H = 16
S5_G = S5_WIDTH // S5_H
S5_P = 64
S5_DT_MIN = 0.001
S5_DT_MAX = 0.1
POOL_WIDTH = D_GROUP
POOL_WINDOWS = (2, 4, 8, 16)
POOL_GC = POOL_WIDTH // len(POOL_WINDOWS)
ATT_HEAD_DIM = 64
ATT_Q_HEADS = D_GROUP // ATT_HEAD_DIM
ATT_KV_HEADS = 2
ATT_Q_PER_KV = ATT_Q_HEADS // ATT_KV_HEADS
ATT_SCALE = 1.0 / math.sqrt(ATT_HEAD_DIM)
ROPE_AXIS_DIM = ATT_HEAD_DIM // 2
ROPE_THETA = 10000.0
Q_BLOCK = 128
D_FF = 4 * D_MODEL
IN_COLS = 3 * HY_WIDTH + S5_WIDTH + POOL_WIDTH + ATT_Q_HEADS * ATT_HEAD_DIM + 2 * ATT_KV_HEADS * ATT_HEAD_DIM
SPLIT_IDX = (3 * HY_WIDTH,
             3 * HY_WIDTH + S5_WIDTH,
             3 * HY_WIDTH + S5_WIDTH + POOL_WIDTH,
             3 * HY_WIDTH + S5_WIDTH + POOL_WIDTH + ATT_Q_HEADS * ATT_HEAD_DIM,
             3 * HY_WIDTH + S5_WIDTH + POOL_WIDTH + (ATT_Q_HEADS + ATT_KV_HEADS) * ATT_HEAD_DIM)

kernel_name = "hybrid_parallel_heads_diffusion_block"


def rmsnorm(x, g):
    xf = x.astype(jnp.float32)
    y = xf * lax.rsqrt(jnp.mean(xf * xf, axis=-1, keepdims=True) + EPS)
    return (y * g.astype(jnp.float32)).astype(x.dtype)


def modulate(h, shift, scale):
    return h * (1 + scale) + shift


def short_conv(u, w, b):
    L = u.shape[1]
    pad = HY_SHORT // 2
    up = jnp.pad(u, ((0, 0), (pad, pad), (0, 0)))
    y = b
    for k in range(HY_SHORT):
        y = y + w[k] * up[:, k:k + L]
    return y


def hyena_kernel_spectra(L, w1, b1, w2, b2, w3, decay):
    f32 = jnp.float32
    t = jnp.arange(L, dtype=f32) / L
    freqs = jnp.arange(1, HY_FREQS + 1, dtype=f32)
    ang = 2.0 * math.pi * t[:, None] * freqs[None, :]
    feats = jnp.concatenate([t[:, None], jnp.cos(ang), jnp.sin(ang)], axis=-1)
    h = jnp.sin(HY_SIN_FREQ * (feats @ w1.astype(f32) + b1.astype(f32)))
    h = jnp.sin(HY_SIN_FREQ * (h @ w2.astype(f32) + b2.astype(f32)))
    h = (h @ w3.astype(f32)).reshape(L, HY_ORDER, 2, HY_WIDTH)
    h = h * jnp.exp(-t[:, None, None, None] * jnp.abs(decay.astype(f32))[None])
    h_fwd, h_bwd = h[:, :, 0], h[:, :, 1]
    kern = jnp.concatenate([h_fwd, jnp.zeros((1, HY_ORDER, HY_WIDTH), f32), h_bwd[:L - 1][::-1]], axis=0)
    return jnp.fft.rfft(kern, axis=0)


def fft_long_conv(z, kern_f):
    L = z.shape[1]
    Z = jnp.fft.rfft(z.astype(jnp.float32), n=2 * L, axis=1)
    return jnp.fft.irfft(Z * kern_f[None], n=2 * L, axis=1)[:, :L]


def hyena_mix(u3, conv_w, conv_b, w1, b1, w2, b2, w3, decay, fbias):
    L = u3.shape[1]
    kern_f = hyena_kernel_spectra(L, w1, b1, w2, b2, w3, decay)
    v, x1, x2 = jnp.split(short_conv(u3, conv_w, conv_b), 3, axis=-1)
    z = v.astype(jnp.float32)
    for o, gate in enumerate((x1, x2)):
        z = gate.astype(jnp.float32) * (fft_long_conv(z, kern_f[:, o]) + fbias[o].astype(jnp.float32) * z)
    return z.astype(u3.dtype)


def s5_zoh(a_re, a_im, log_dt, b_re, b_im):
    f32 = jnp.float32
    a_re, a_im = a_re.astype(f32), a_im.astype(f32)
    dt = jnp.exp(log_dt.astype(f32))[:, None]
    mag = jnp.exp(a_re * dt)
    lam_re, lam_im = mag * jnp.cos(a_im * dt), mag * jnp.sin(a_im * dt)
    den = a_re * a_re + a_im * a_im
    nr, ni = lam_re - 1.0, lam_im
    cr = (nr * a_re + ni * a_im) / den
    ci = (ni * a_re - nr * a_im) / den
    b_re, b_im = b_re.astype(f32), b_im.astype(f32)
    bb_re = cr[..., None] * b_re - ci[..., None] * b_im
    bb_im = cr[..., None] * b_im + ci[..., None] * b_re
    return lam_re, lam_im, bb_re, bb_im


def _complex_affine_combine(e1, e2):
    a1r, a1i, b1r, b1i = e1
    a2r, a2i, b2r, b2i = e2
    return (a2r * a1r - a2i * a1i,
            a2r * a1i + a2i * a1r,
            a2r * b1r - a2i * b1i + b2r,
            a2r * b1i + a2i * b1r + b2i)


def diag_scan(lam_re, lam_im, bu_re, bu_im, h0_re, h0_im, reverse):
    if h0_re is not None:
        edge = -1 if reverse else 0
        bu_re = bu_re.at[:, edge].add(lam_re * h0_re - lam_im * h0_im)
        bu_im = bu_im.at[:, edge].add(lam_re * h0_im + lam_im * h0_re)
    a_re = jnp.broadcast_to(lam_re, bu_re.shape)
    a_im = jnp.broadcast_to(lam_im, bu_im.shape)
    _, _, h_re, h_im = lax.associative_scan(_complex_affine_combine, (a_re, a_im, bu_re, bu_im), reverse=reverse, axis=1)
    return h_re, h_im


def s5_readout(h_re, h_im, c_re, c_im):
    B, L = h_re.shape[:2]
    y = jnp.einsum('blgp,ghp->blgh', h_re, c_re.astype(jnp.float32)) - jnp.einsum('blgp,ghp->blgh', h_im, c_im.astype(jnp.float32))
    return y.reshape(B, L, S5_WIDTH)


def s5_glu(y, w, b):
    g = jax.nn.gelu(y)
    return g * jax.nn.sigmoid(g @ w.astype(jnp.float32) + b.astype(jnp.float32))


def s5_mix(u_l, u_c, a_re, a_im, log_dt, b_re, b_im, c_re, c_im, d, glu_w, glu_b, need_ctx_out):
    B, L, _ = u_l.shape
    Lc = u_c.shape[1]
    ul = u_l.astype(jnp.float32)
    uc = u_c.astype(jnp.float32)
    ulg = ul.reshape(B, L, S5_G, S5_H)
    ucg = uc.reshape(B, Lc, S5_G, S5_H)
    dd = d.astype(jnp.float32)
    y_l = dd * ul
    y_c = dd * uc
    for direction in range(2):
        rev = direction == 1
        lam_re, lam_im, bb_re, bb_im = s5_zoh(a_re[direction], a_im[direction], log_dt[direction], b_re[direction], b_im[direction])
        hc_re, hc_im = diag_scan(lam_re, lam_im,
                                 jnp.einsum('blgh,gph->blgp', ucg, bb_re), jnp.einsum('blgh,gph->blgp', ucg, bb_im),
                                 None, None, rev)
        edge = 0 if rev else -1
        hl_re, hl_im = diag_scan(lam_re, lam_im,
                                 jnp.einsum('blgh,gph->blgp', ulg, bb_re), jnp.einsum('blgh,gph->blgp', ulg, bb_im),
                                 hc_re[:, edge], hc_im[:, edge], rev)
        y_l = y_l + s5_readout(hl_re, hl_im, c_re[direction], c_im[direction])
        if need_ctx_out:
            y_c = y_c + s5_readout(hc_re, hc_im, c_re[direction], c_im[direction])
    out_l = s5_glu(y_l, glu_w, glu_b).astype(u_l.dtype)
    out_c = s5_glu(y_c, glu_w, glu_b).astype(u_c.dtype) if need_ctx_out else None
    return out_l, out_c


def pool_mix(u, w, scale):
    B, L, W = u.shape
    uf = u.astype(jnp.float32)
    cs = jnp.concatenate([jnp.zeros((B, 1, W), jnp.float32), jnp.cumsum(uf, axis=1)], axis=1)
    t = jnp.arange(L)
    outs = []
    for g, win in enumerate(POOL_WINDOWS):
        csg = cs[..., g * POOL_GC:(g + 1) * POOL_GC]
        lo = jnp.clip(t - win // 2, 0, L)
        hi = jnp.clip(t + win // 2, 0, L)
        mean = (jnp.take(csg, hi, axis=1) - jnp.take(csg, lo, axis=1)) / (hi - lo).astype(jnp.float32)[None, :, None]
        outs.append(mean - uf[..., g * POOL_GC:(g + 1) * POOL_GC])
    pooled = jnp.stack(outs, axis=2)
    mixed = jnp.einsum('blgc,gcd->blgd', pooled, w.astype(jnp.float32)).reshape(B, L, W)
    return (mixed * scale.astype(jnp.float32)).astype(u.dtype)


def axial_rope_tables(rows, cols):
    inv = ROPE_THETA ** (-jnp.arange(0, ROPE_AXIS_DIM, 2, dtype=jnp.float32) / ROPE_AXIS_DIM)
    ang = jnp.concatenate([rows.astype(jnp.float32)[:, None] * inv[None, :],
                           cols.astype(jnp.float32)[:, None] * inv[None, :]], axis=-1)
    return jnp.cos(ang), jnp.sin(ang)


def apply_rope(x, cos, sin):
    xp = x.astype(jnp.float32).reshape(*x.shape[:-1], ATT_HEAD_DIM // 2, 2)
    x0, x1 = xp[..., 0], xp[..., 1]
    c = cos[None, :, None, :]
    s = sin[None, :, None, :]
    return jnp.stack([x0 * c - x1 * s, x0 * s + x1 * c], axis=-1).reshape(x.shape).astype(x.dtype)


def gqa_softmax(q, k, v):
    s = jnp.einsum('bqkgd,bskd->bkgqs', q, k).astype(jnp.float32) * ATT_SCALE
    p = jax.nn.softmax(s, axis=-1).astype(v.dtype)
    return jnp.einsum('bkgqs,bskd->bqkgd', p, v)


def attention_mix(q_l, k_l, v_l, q_c, k_c, v_c, q_gain, k_gain, rope_cos, rope_sin, need_ctx_out):
    B, L, _ = q_l.shape
    Lc = k_c.shape[1]
    heads = lambda t, h: t.reshape(t.shape[0], t.shape[1], h, ATT_HEAD_DIM)
    ql = apply_rope(rmsnorm(heads(q_l, ATT_Q_HEADS), q_gain), rope_cos, rope_sin)
    kl = apply_rope(rmsnorm(heads(k_l, ATT_KV_HEADS), k_gain), rope_cos, rope_sin)
    kc = rmsnorm(heads(k_c, ATT_KV_HEADS), k_gain)
    vl, vc = heads(v_l, ATT_KV_HEADS), heads(v_c, ATT_KV_HEADS)
    k_all = jnp.concatenate([kl, kc], axis=1)
    v_all = jnp.concatenate([vl, vc], axis=1)
    qb = ql.reshape(B, L // Q_BLOCK, Q_BLOCK, ATT_KV_HEADS, ATT_Q_PER_KV, ATT_HEAD_DIM).transpose(1, 0, 2, 3, 4, 5)
    ob = lax.map(lambda q: gqa_softmax(q, k_all, v_all), qb)
    y_l = ob.transpose(1, 0, 2, 3, 4, 5).reshape(B, L, ATT_Q_HEADS * ATT_HEAD_DIM)
    y_c = None
    if need_ctx_out:
        qc = rmsnorm(heads(q_c, ATT_Q_HEADS), q_gain).reshape(B, Lc, ATT_KV_HEADS, ATT_Q_PER_KV, ATT_HEAD_DIM)
        y_c = gqa_softmax(qc, kc, vc).reshape(B, Lc, ATT_Q_HEADS * ATT_HEAD_DIM)
    return y_l, y_c


def sq_relu_mlp(h, w1, w2):
    return jnp.square(jax.nn.relu(h @ w1)) @ w2


def token_mix_one_stream(a, s5_y, p, att_y, lp):
    y_hy = hyena_mix(a, lp["hy_conv_w"], lp["hy_conv_b"], lp["hy_ffn_w1"], lp["hy_ffn_b1"],
                     lp["hy_ffn_w2"], lp["hy_ffn_b2"], lp["hy_ffn_w3"], lp["hy_decay"], lp["hy_bias"])
    y_pool = pool_mix(p, lp["pool_w"], lp["pool_scale"])
    merged = jnp.concatenate([y_hy.astype(a.dtype), s5_y.astype(a.dtype), y_pool.astype(a.dtype), att_y.astype(a.dtype)], axis=-1)
    return merged @ lp["w_out"]


def layer(xl, xc, c, c_ctx, rope_cos, rope_sin, lp, need_ctx_out):
    mod_l = jnp.split((jax.nn.silu(c) @ lp["mod_w"] + lp["mod_b"])[:, None, :], N_MOD, axis=-1)
    mod_c = jnp.split(jax.nn.silu(c_ctx) @ lp["mod_w"] + lp["mod_b"], N_MOD, axis=-1)
    hl = modulate(rmsnorm(xl, lp["norm_pre_mix"]), mod_l[0], mod_l[1])
    hc = modulate(rmsnorm(xc, lp["norm_pre_mix"]), mod_c[0], mod_c[1])
    a_l, s_l, p_l, q_l, k_l, v_l = jnp.split(hl @ lp["w_in"], SPLIT_IDX, axis=-1)
    a_c, s_c, p_c, q_c, k_c, v_c = jnp.split(hc @ lp["w_in"], SPLIT_IDX, axis=-1)
    y_s5_l, y_s5_c = s5_mix(s_l, s_c, lp["s5_a_re"], lp["s5_a_im"], lp["s5_log_dt"], lp["s5_b_re"], lp["s5_b_im"],
                            lp["s5_c_re"], lp["s5_c_im"], lp["s5_d"], lp["s5_glu_w"], lp["s5_glu_b"], need_ctx_out)
    y_att_l, y_att_c = attention_mix(q_l, k_l, v_l, q_c, k_c, v_c, lp["att_q_norm"], lp["att_k_norm"],
                                     rope_cos, rope_sin, need_ctx_out)
    ol = token_mix_one_stream(a_l, y_s5_l, p_l, y_att_l, lp)
    xl = xl + mod_l[2] * rmsnorm(ol, lp["norm_post_mix"])
    fl = sq_relu_mlp(modulate(rmsnorm(xl, lp["norm_pre_mlp"]), mod_l[3], mod_l[4]), lp["mlp_w1"], lp["mlp_w2"])
    xl = xl + mod_l[5] * rmsnorm(fl, lp["norm_post_mlp"])
    if not need_ctx_out:
        return xl, xc
    oc = token_mix_one_stream(a_c, y_s5_c, p_c, y_att_c, lp)
    xc = xc + mod_c[2] * rmsnorm(oc, lp["norm_post_mix"])
    fc = sq_relu_mlp(modulate(rmsnorm(xc, lp["norm_pre_mlp"]), mod_c[3], mod_c[4]), lp["mlp_w1"], lp["mlp_w2"])
    xc = xc + mod_c[5] * rmsnorm(fc, lp["norm_post_mlp"])
    return xl, xc


def setup_inputs(seed: int = 0) -> dict:
    key = jax.random.key(seed)
    ks = iter(jax.random.split(key, 64))
    f32 = jnp.float32

    def nrm(shape, scale):
        return scale * jax.random.normal(next(ks), shape, f32)

    def gain(shape):
        return 1.0 + nrm(shape, 0.05)

    x = nrm((BATCH, SEQ, D_MODEL), 1.0)
    c = nrm((BATCH, D_MODEL), 1.0)
    ctx = nrm((BATCH, CTX_LEN, D_MODEL), 1.0)
    c_ctx = nrm((D_MODEL,), 1.0)
    mod_w = nrm((DEPTH, D_MODEL, N_MOD * D_MODEL), 0.5 * D_MODEL ** -0.5)
    mod_b = nrm((DEPTH, N_MOD * D_MODEL), 0.02)
    norm_pre_mix = gain((DEPTH, D_MODEL))
    norm_post_mix = gain((DEPTH, D_MODEL))
    norm_pre_mlp = gain((DEPTH, D_MODEL))
    norm_post_mlp = gain((DEPTH, D_MODEL))
    w_in = nrm((DEPTH, D_MODEL, IN_COLS), D_MODEL ** -0.5)
    w_out = nrm((DEPTH, D_MIX, D_MODEL), D_MIX ** -0.5)
    hy_conv_w = nrm((DEPTH, HY_SHORT, 3 * HY_WIDTH), HY_SHORT ** -0.5)
    hy_conv_b = nrm((DEPTH, 3 * HY_WIDTH), 0.02)
    hy_ffn_w1 = nrm((DEPTH, HY_EMB, HY_FFN), HY_EMB ** -0.5)
    hy_ffn_b1 = nrm((DEPTH, HY_FFN), 0.1)
    hy_ffn_w2 = nrm((DEPTH, HY_FFN, HY_FFN), HY_FFN ** -0.5)
    hy_ffn_b2 = nrm((DEPTH, HY_FFN), 0.1)
    hy_ffn_w3 = nrm((DEPTH, HY_FFN, HY_ORDER * 2 * HY_WIDTH), 0.05 * HY_FFN ** -0.5)
    hy_decay = jnp.linspace(HY_DECAY_MIN, HY_DECAY_MAX, HY_WIDTH, dtype=f32)[None, None, None, :] + nrm((DEPTH, HY_ORDER, 2, HY_WIDTH), 0.1)
    hy_bias = nrm((DEPTH, HY_ORDER, HY_WIDTH), 0.1)
    s5_a_re = -0.5 + nrm((DEPTH, 2, S5_G, S5_P), 0.01)
    s5_a_im = math.pi * jnp.arange(S5_P, dtype=f32)[None, None, None, :] + nrm((DEPTH, 2, S5_G, S5_P), 0.01)
    s5_log_dt = math.log(S5_DT_MIN) + jax.random.uniform(next(ks), (DEPTH, 2, S5_G), f32) * (math.log(S5_DT_MAX) - math.log(S5_DT_MIN))
    s5_b_re = nrm((DEPTH, 2, S5_G, S5_P, S5_H), (2 * S5_H) ** -0.5)
    s5_b_im = nrm((DEPTH, 2, S5_G, S5_P, S5_H), (2 * S5_H) ** -0.5)
    s5_c_re = nrm((DEPTH, 2, S5_G, S5_H, S5_P), (2 * S5_P) ** -0.5)
    s5_c_im = nrm((DEPTH, 2, S5_G, S5_H, S5_P), (2 * S5_P) ** -0.5)
    s5_d = nrm((DEPTH, S5_WIDTH), 0.5)
    s5_glu_w = nrm((DEPTH, S5_WIDTH, S5_WIDTH), S5_WIDTH ** -0.5)
    s5_glu_b = nrm((DEPTH, S5_WIDTH), 0.02)
    pool_w = nrm((DEPTH, len(POOL_WINDOWS), POOL_GC, POOL_GC), POOL_GC ** -0.5)
    pool_scale = 1.0 + nrm((DEPTH, POOL_WIDTH), 0.1)
    att_q_norm = gain((DEPTH, ATT_HEAD_DIM))
    att_k_norm = gain((DEPTH, ATT_HEAD_DIM))
    mlp_w1 = nrm((DEPTH, D_MODEL, D_FF), D_MODEL ** -0.5)
    mlp_w2 = nrm((DEPTH, D_FF, D_MODEL), D_FF ** -0.5)
    return {"x": x, "c": c, "ctx": ctx, "c_ctx": c_ctx, "mod_w": mod_w, "mod_b": mod_b,
            "norm_pre_mix": norm_pre_mix, "norm_post_mix": norm_post_mix,
            "norm_pre_mlp": norm_pre_mlp, "norm_post_mlp": norm_post_mlp,
            "w_in": w_in, "w_out": w_out, "hy_conv_w": hy_conv_w, "hy_conv_b": hy_conv_b,
            "hy_ffn_w1": hy_ffn_w1, "hy_ffn_b1": hy_ffn_b1, "hy_ffn_w2": hy_ffn_w2, "hy_ffn_b2": hy_ffn_b2,
            "hy_ffn_w3": hy_ffn_w3, "hy_decay": hy_decay, "hy_bias": hy_bias,
            "s5_a_re": s5_a_re, "s5_a_im": s5_a_im, "s5_log_dt": s5_log_dt,
            "s5_b_re": s5_b_re, "s5_b_im": s5_b_im, "s5_c_re": s5_c_re, "s5_c_im": s5_c_im,
            "s5_d": s5_d, "s5_glu_w": s5_glu_w, "s5_glu_b": s5_glu_b,
            "pool_w": pool_w, "pool_scale": pool_scale, "att_q_norm": att_q_norm, "att_k_norm": att_k_norm,
            "mlp_w1": mlp_w1, "mlp_w2": mlp_w2}


def reference(x, c, ctx, c_ctx, mod_w, mod_b, norm_pre_mix, norm_post_mix, norm_pre_mlp, norm_post_mlp,
              w_in, w_out, hy_conv_w, hy_conv_b, hy_ffn_w1, hy_ffn_b1, hy_ffn_w2, hy_ffn_b2, hy_ffn_w3,
              hy_decay, hy_bias, s5_a_re, s5_a_im, s5_log_dt, s5_b_re, s5_b_im, s5_c_re, s5_c_im, s5_d,
              s5_glu_w, s5_glu_b, pool_w, pool_scale, att_q_norm, att_k_norm, mlp_w1, mlp_w2):
    L = x.shape[1]
    ROWS = L // GRID_W
    rows = jnp.repeat(jnp.arange(ROWS, dtype=jnp.int32), GRID_W)
    cols = jnp.tile(jnp.arange(GRID_W, dtype=jnp.int32), ROWS)
    rope_cos, rope_sin = axial_rope_tables(rows, cols)
    xl, xc = x, ctx
    for i in range(DEPTH):
        lp = {"mod_w": mod_w[i], "mod_b": mod_b[i],
              "norm_pre_mix": norm_pre_mix[i], "norm_post_mix": norm_post_mix[i],
              "norm_pre_mlp": norm_pre_mlp[i], "norm_post_mlp": norm_post_mlp[i],
              "w_in": w_in[i], "w_out": w_out[i], "hy_conv_w": hy_conv_w[i], "hy_conv_b": hy_conv_b[i],
              "hy_ffn_w1": hy_ffn_w1[i], "hy_ffn_b1": hy_ffn_b1[i], "hy_ffn_w2": hy_ffn_w2[i],
              "hy_ffn_b2": hy_ffn_b2[i], "hy_ffn_w3": hy_ffn_w3[i], "hy_decay": hy_decay[i], "hy_bias": hy_bias[i],
              "s5_a_re": s5_a_re[i], "s5_a_im": s5_a_im[i], "s5_log_dt": s5_log_dt[i],
              "s5_b_re": s5_b_re[i], "s5_b_im": s5_b_im[i], "s5_c_re": s5_c_re[i], "s5_c_im": s5_c_im[i],
              "s5_d": s5_d[i], "s5_glu_w": s5_glu_w[i], "s5_glu_b": s5_glu_b[i],
              "pool_w": pool_w[i], "pool_scale": pool_scale[i],
              "att_q_norm": att_q_norm[i], "att_k_norm": att_k_norm[i],
              "mlp_w1": mlp_w1[i], "mlp_w2": mlp_w2[i]}
        xl, xc = layer(xl, xc, c, c_ctx, rope_cos, rope_sin, lp, i < DEPTH - 1)
    return xl
```

```python
import functools
import math

import jax
import jax.numpy as jnp
from jax import lax
from jax.experimental import pallas as pl
from jax.experimental.pallas import tpu as pltpu

F32 = jnp.float32
BF16 = jnp.bfloat16

D = 1024
L = 16384
LC = 256
LT = L + LC
DEPTH = 4
GRID_W = 64
EPS = 1e-6
N_MOD = 6
DG = 256
HY_ORDER = 2
HY_FREQS = 16
HY_FFN = 64
HY_SIN_FREQ = 1.0
S5_H = 16
S5_G = 16
S5_P = 64
HEAD = 64
QH = 4
KVH = 2
ATT_SCALE = 1.0 / math.sqrt(HEAD)
ROPE_THETA = 10000.0
D_FF = 4 * D
IN_COLS = 1792

TR = 256
NLB = L // TR
NRB = LT // TR
HALO = 8

FFT_N = 2 * L
FFT_N1 = 256
FFT_N2 = 128
FFT_K1 = FFT_N1 // 2
FFT_R = L // FFT_N2
FFT_TC = 4096
FFT_TK = 8

S5_T = 8
S5_W = S5_T * DG
S5_ROWS = LT // S5_T
S5_SB = LC // S5_T
S5_NSB = S5_ROWS // S5_SB
S5_TN = 512

ATT_TQ = 512
ATT_TK = 512

VMEM_LIMIT = 56 * 1024 * 1024


def _params(*sem):
    return pltpu.CompilerParams(dimension_semantics=sem, vmem_limit_bytes=VMEM_LIMIT)


def _const_spec(shape):
    nd = len(shape)
    return pl.BlockSpec(shape, lambda *_: (0,) * nd, pipeline_mode=pl.Buffered(1))


def _bdot(a, b):
    return jnp.dot(a.astype(BF16), b.astype(BF16), preferred_element_type=F32)


def _split(a):
    hi = a.astype(BF16)
    lo = (a - hi.astype(F32)).astype(BF16)
    return hi, lo


def _dot3(a, b):
    ah, al = _split(a)
    bh, bl = _split(b)
    return (jnp.dot(ah, bh, preferred_element_type=F32)
            + jnp.dot(al, bh, preferred_element_type=F32)
            + jnp.dot(ah, bl, preferred_element_type=F32))


def _rms(x, g):
    return x * lax.rsqrt(jnp.mean(x * x, axis=-1, keepdims=True) + EPS) * g


def _mod_kernel(c_ref, w_ref, b_ref, o_ref):
    c = c_ref[...]
    s = c * jax.nn.sigmoid(c)
    o_ref[0] = _dot3(s, w_ref[0]) + b_ref[0]


def _modulation(c, c_ctx, mod_w, mod_b):
    cc = jnp.concatenate([c.reshape(1, D), c_ctx.reshape(1, D), jnp.zeros((6, D), F32)], axis=0)
    tn = 1536
    out = pl.pallas_call(
        _mod_kernel,
        grid=(DEPTH, N_MOD * D // tn),
        in_specs=[pl.BlockSpec((8, D), lambda i, j: (0, 0)),
                  pl.BlockSpec((1, D, tn), lambda i, j: (i, 0, j)),
                  pl.BlockSpec((1, 1, tn), lambda i, j: (i, 0, j))],
        out_specs=pl.BlockSpec((1, 8, tn), lambda i, j: (i, 0, j)),
        out_shape=jax.ShapeDtypeStruct((DEPTH, 8, N_MOD * D), F32),
        compiler_params=_params("parallel", "parallel"),
        name="modulation",
    )(cc, mod_w, mod_b.reshape(DEPTH, 1, N_MOD * D))
    return out[:, :2].reshape(DEPTH, 2, N_MOD, D)


def _qknorm_rope(t, gain, cs, sn, hm):
    hi, lo = _split(t * t)
    ms = jnp.dot(hi, hm, preferred_element_type=F32) + jnp.dot(lo, hm, preferred_element_type=F32)
    tn = t * lax.rsqrt(ms + EPS) * gain
    w = t.shape[1]
    lane = lax.broadcasted_iota(jnp.int32, tn.shape, 1)
    swapped = jnp.where((lane % HEAD) < HEAD // 2, pltpu.roll(tn, w - HEAD // 2, 1), pltpu.roll(tn, HEAD // 2, 1))
    return tn * cs + swapped * sn


def _premix_kernel(x_ref, mod_ref, g_ref, w_ref, cs_ref, sn_ref, qg_ref, kg_ref, hm_ref,
                   a_ref, s_ref, p_ref, q_ref, k_ref, v_ref):
    m = mod_ref[0]
    h = _rms(x_ref[...], g_ref[...]) * (1.0 + m[1:2]) + m[0:1]
    u = jnp.dot(h.astype(BF16), w_ref[...], preferred_element_type=F32)
    a_ref[...] = u[:, 0:768]
    s_ref[...] = u[:, 768:1024].astype(BF16)
    p_ref[...] = u[:, 1024:1280]
    cs = cs_ref[...]
    sn = sn_ref[...]
    hm = hm_ref[...]
    q = _qknorm_rope(u[:, 1280:1536], qg_ref[...], jnp.concatenate([cs, cs], axis=1),
                     jnp.concatenate([sn, sn], axis=1), hm) * ATT_SCALE
    k = _qknorm_rope(u[:, 1536:1664], kg_ref[...], cs, sn, hm[:128, :128])
    v = u[:, 1664:1792]
    for hh in range(QH):
        q_ref[hh] = q[:, HEAD * hh:HEAD * (hh + 1)].astype(BF16)
    for hh in range(KVH):
        k_ref[hh] = k[:, HEAD * hh:HEAD * (hh + 1)].astype(BF16)
        v_ref[hh] = v[:, HEAD * hh:HEAD * (hh + 1)].astype(BF16)


def _premix(x, mods, g, w_in, cs, sn, qg, kg, hm):
    row = lambda w: pl.BlockSpec((TR, w), lambda i: (i, 0))
    return pl.pallas_call(
        _premix_kernel,
        grid=(NRB,),
        in_specs=[row(D),
                  pl.BlockSpec((1, N_MOD, D), lambda i: (i // NLB, 0, 0)),
                  _const_spec((1, D)), _const_spec((D, IN_COLS)),
                  row(128), row(128),
                  _const_spec((1, 256)), _const_spec((1, 128)), _const_spec((256, 256))],
        out_specs=[row(768), row(256), row(256),
                   pl.BlockSpec((QH, TR, HEAD), lambda i: (0, i, 0)),
                   pl.BlockSpec((KVH, TR, HEAD), lambda i: (0, i, 0)),
                   pl.BlockSpec((KVH, TR, HEAD), lambda i: (0, i, 0))],
        out_shape=[jax.ShapeDtypeStruct((LT, 768), F32),
                   jax.ShapeDtypeStruct((LT, 256), BF16),
                   jax.ShapeDtypeStruct((LT, 256), F32),
                   jax.ShapeDtypeStruct((QH, LT, HEAD), BF16),
                   jax.ShapeDtypeStruct((KVH, LT, HEAD), BF16),
                   jax.ShapeDtypeStruct((KVH, LT, HEAD), BF16)],
        compiler_params=_params("parallel"),
        name="premix",
    )(x, mods, g, w_in, cs, sn, qg, kg, hm)


def _local_kernel(a_ref, ap_ref, an_ref, p_ref, pp_ref, pn_ref, cw_ref, cb_ref, pw_ref, ps_ref,
                  v_ref, x1_ref, x2_ref, po_ref):
    i = pl.program_id(0)
    is_ctx = i >= NLB
    s0 = jnp.where(is_ctx, L, 0)
    s1 = jnp.where(is_ctx, LT, L)
    n = TR + 2 * HALO
    idx = i * TR - HALO + lax.broadcasted_iota(jnp.int32, (n, 1), 0)
    valid = (idx >= s0) & (idx < s1)

    def shifted(e, d):
        return pltpu.roll(e, (-d) % n, 0)[HALO:HALO + TR] if d else e[HALO:HALO + TR]

    ea = jnp.where(valid, jnp.concatenate([ap_ref[...], a_ref[...], an_ref[...]], axis=0), 0.0)
    w = cw_ref[...]
    y = cb_ref[...] + w[0:1] * shifted(ea, -1) + w[1:2] * shifted(ea, 0) + w[2:3] * shifted(ea, 1)
    v_ref[...] = y[:, 0:256]
    x1_ref[...] = y[:, 256:512]
    x2_ref[...] = y[:, 512:768]

    e = jnp.where(valid, jnp.concatenate([pp_ref[...], p_ref[...], pn_ref[...]], axis=0), 0.0)
    w2 = e + pltpu.roll(e, 1, 0)
    w4 = pltpu.roll(w2, 1, 0) + pltpu.roll(w2, n - 1, 0)
    w8 = pltpu.roll(w4, 2, 0) + pltpu.roll(w4, n - 2, 0)
    w16 = pltpu.roll(w8, 4, 0) + pltpu.roll(w8, n - 4, 0)
    lane = lax.broadcasted_iota(jnp.int32, (TR, 256), 1)
    cut = lambda t: t[HALO:HALO + TR]
    sums = jnp.where(lane < 64, cut(w2), jnp.where(lane < 128, cut(w4), jnp.where(lane < 192, cut(w8), cut(w16))))
    half = jnp.where(lane < 64, 1, jnp.where(lane < 128, 2, jnp.where(lane < 192, 4, 8)))
    tl = i * TR + lax.broadcasted_iota(jnp.int32, (TR, 256), 0) - s0
    cnt = jnp.minimum(tl + half, s1 - s0) - jnp.maximum(tl - half, 0)
    pooled = sums / cnt.astype(F32) - p_ref[...]
    po_ref[...] = _bdot(pooled, pw_ref[...]) * ps_ref[...]


def _local(a, p, conv_w, conv_b, pool_bd, pool_scale):
    r8 = TR // HALO
    main = lambda w: pl.BlockSpec((TR, w), lambda i: (i, 0))
    prev = lambda w: pl.BlockSpec((HALO, w), lambda i: (jnp.maximum(i * r8 - 1, 0), 0))
    nxt = lambda w: pl.BlockSpec((HALO, w), lambda i: (jnp.minimum((i + 1) * r8, LT // HALO - 1), 0))
    o = jax.ShapeDtypeStruct((LT, 256), F32)
    return pl.pallas_call(
        _local_kernel,
        grid=(NRB,),
        in_specs=[main(768), prev(768), nxt(768), main(256), prev(256), nxt(256),
                  _const_spec((3, 768)), _const_spec((1, 768)), _const_spec((256, 256)), _const_spec((1, 256))],
        out_specs=[main(256)] * 4,
        out_shape=[o, o, o, o],
        compiler_params=_params("parallel"),
        name="local_ops",
    )(a, a, a, p, p, p, conv_w, conv_b, pool_bd, pool_scale)


def _filter_hidden(n, inv_len, w1, b1, w2, b2):
    t = n * inv_len
    lane = lax.broadcasted_iota(jnp.int32, (n.shape[0], 128), 1)
    fr = jnp.where(lane <= HY_FREQS, lane, lane - HY_FREQS).astype(F32)
    ang = (2.0 * math.pi * t) * fr
    feats = jnp.where(lane == 0, t,
                      jnp.where(lane <= HY_FREQS, jnp.cos(ang),
                                jnp.where(lane <= 2 * HY_FREQS, jnp.sin(ang), 0.0)))
    h = jnp.sin(HY_SIN_FREQ * (_dot3(feats, w1) + b1))
    h = jnp.sin(HY_SIN_FREQ * (_dot3(h, w2) + b2))
    return h, t


def _filter_block(n, inv_len, w1, b1, w2, b2, w3f, w3b, dec_f, dec_b):
    h, t = _filter_hidden(n, inv_len, w1, b1, w2, b2)
    hf = _dot3(h, w3f) * jnp.exp(-t * jnp.abs(dec_f))
    hp, tp = _filter_hidden(n - 1.0, inv_len, w1, b1, w2, b2)
    hb = _dot3(hp, w3b) * jnp.exp(-tp * jnp.abs(dec_b))
    hb = jnp.where(n >= 1.0, hb, 0.0)
    return hf, hb


def _filter_kernel(w1_ref, b1_ref, w2_ref, b2_ref, w3f_ref, w3b_ref, df_ref, db_ref, o_ref, *, rows):
    i = pl.program_id(0)
    n = (i * rows + lax.broadcasted_iota(jnp.int32, (rows, 1), 0)).astype(F32)
    hf, hb = _filter_block(n, 1.0 / L, w1_ref[...], b1_ref[...], w2_ref[...], b2_ref[...],
                           w3f_ref[...], w3b_ref[...], df_ref[...], db_ref[...])
    o_ref[:, 0:256] = hf[:, 0:256]
    o_ref[:, 256:512] = hb[:, 0:256]
    o_ref[:, 512:768] = hf[:, 256:512]
    o_ref[:, 768:1024] = hb[:, 256:512]


def _filter_gen(fp):
    rows = 512
    return pl.pallas_call(
        functools.partial(_filter_kernel, rows=rows),
        grid=(L // rows,),
        in_specs=[_const_spec((128, HY_FFN)), _const_spec((1, HY_FFN)), _const_spec((HY_FFN, HY_FFN)),
                  _const_spec((1, HY_FFN)), _const_spec((HY_FFN, 512)), _const_spec((HY_FFN, 512)),
                  _const_spec((1, 512)), _const_spec((1, 512))],
        out_specs=pl.BlockSpec((rows, 1024), lambda i: (i, 0)),
        out_shape=jax.ShapeDtypeStruct((L, 1024), F32),
        compiler_params=_params("parallel"),
        name="hyena_filter",
    )(*fp)


def _fft_fwd1_kernel(x_ref, m_ref, o_ref):
    o_ref[...] = jnp.dot(m_ref[...], x_ref[...].astype(BF16), preferred_element_type=F32).astype(BF16)


def _fft_fwd1(x2d, m1):
    cols = x2d.shape[1]
    return pl.pallas_call(
        _fft_fwd1_kernel,
        grid=(cols // FFT_TC,),
        in_specs=[pl.BlockSpec((FFT_R, FFT_TC), lambda j: (0, j)), _const_spec((FFT_N1, FFT_R))],
        out_specs=pl.BlockSpec((FFT_N1, FFT_TC), lambda j: (0, j)),
        out_shape=jax.ShapeDtypeStruct((FFT_N1, cols), BF16),
        compiler_params=_params("parallel"),
        name="fft_fwd1",
    )(x2d, m1)


def _filt_spec_kernel(a_ref, w_ref, o_ref):
    for j in range(FFT_TK):
        x = jnp.dot(w_ref[j], jnp.concatenate([a_ref[0, j], a_ref[1, j]], axis=0), preferred_element_type=F32)
        o_ref[0, 0, j] = x[:FFT_N2, 0:256] + x[:FFT_N2, 256:512]
        o_ref[0, 1, j] = x[FFT_N2:, 0:256] - x[FFT_N2:, 256:512]


def _filt_spec(fa, w):
    return pl.pallas_call(
        _filt_spec_kernel,
        grid=(HY_ORDER, FFT_K1 // FFT_TK),
        in_specs=[pl.BlockSpec((2, FFT_TK, FFT_N2, 512), lambda o, i: (0, i, 0, o)),
                  pl.BlockSpec((FFT_TK, 256, 256), lambda o, i: (i, 0, 0))],
        out_specs=pl.BlockSpec((1, 2, FFT_TK, FFT_N2, 256), lambda o, i: (o, 0, i, 0, 0)),
        out_shape=jax.ShapeDtypeStruct((HY_ORDER, 2, FFT_K1, FFT_N2, 256), F32),
        compiler_params=_params("parallel", "parallel"),
        name="hyena_filter_spectrum",
    )(fa, w)


def _fft_mid_kernel(a_ref, w_ref, wt_ref, k_ref, o_ref):
    for j in range(FFT_TK):
        x = jnp.dot(w_ref[j], jnp.concatenate([a_ref[0, j], a_ref[1, j]], axis=0), preferred_element_type=F32)
        xr, xi = x[:FFT_N2], x[FFT_N2:]
        kr, ki = k_ref[0, 0, j], k_ref[0, 1, j]
        y = jnp.concatenate([xr * kr - xi * ki, xr * ki + xi * kr], axis=0).astype(BF16)
        b = jnp.dot(wt_ref[j], y, preferred_element_type=F32)
        o_ref[0, j] = b[:FFT_N2].astype(BF16)
        o_ref[1, j] = b[FFT_N2:].astype(BF16)


def _fft_mid(a, w, wt, kf, order):
    blk = pl.BlockSpec((2, FFT_TK, FFT_N2, 256), lambda i: (0, i, 0, 0))
    wblk = pl.BlockSpec((FFT_TK, 256, 256), lambda i: (i, 0, 0))
    return pl.pallas_call(
        _fft_mid_kernel,
        grid=(FFT_K1 // FFT_TK,),
        in_specs=[blk, wblk, wblk,
                  pl.BlockSpec((1, 2, FFT_TK, FFT_N2, 256), lambda i: (order, 0, i, 0, 0))],
        out_specs=blk,
        out_shape=jax.ShapeDtypeStruct((2, FFT_K1, FFT_N2, 256), BF16),
        compiler_params=_params("parallel"),
        name="fft_mid",
    )(a, w, wt, kf)


def _fft_inv1_kernel(b_ref, m_ref, x_ref, z_ref, fb_ref, o_ref):
    y = jnp.dot(m_ref[...], b_ref[...], preferred_element_type=F32)
    z = z_ref[...]
    o_ref[...] = x_ref[...] * (y + fb_ref[...] * z)


def _fft_inv1(b2d, minv, gate2d, z2d, fb_tiled):
    cols = b2d.shape[1]
    sig = pl.BlockSpec((FFT_R, FFT_TC), lambda j: (0, j))
    return pl.pallas_call(
        _fft_inv1_kernel,
        grid=(cols // FFT_TC,),
        in_specs=[pl.BlockSpec((FFT_N1, FFT_TC), lambda j: (0, j)), _const_spec((FFT_R, FFT_N1)),
                  sig, sig, _const_spec((1, FFT_TC))],
        out_specs=sig,
        out_shape=jax.ShapeDtypeStruct((FFT_R, cols), F32),
        compiler_params=_params("parallel"),
        name="fft_inv1_gate",
    )(b2d, minv, gate2d, z2d, fb_tiled)


def _ctx_hyena_kernel(a_ref, cw_ref, cb_ref, w1_ref, b1_ref, w2_ref, b2_ref, w3f_ref, w3b_ref, df_ref, db_ref,
                      fb_ref, fc_ref, fi_ref, o_ref):
    zero = jnp.zeros((HALO, 768), F32)
    n = LC + 2 * HALO
    ea = jnp.concatenate([zero, a_ref[...], zero], axis=0)
    w = cw_ref[...]
    y = (cb_ref[...] + w[0:1] * pltpu.roll(ea, 1, 0)[HALO:HALO + LC] + w[1:2] * ea[HALO:HALO + LC]
         + w[2:3] * pltpu.roll(ea, n - 1, 0)[HALO:HALO + LC])
    pos = lax.broadcasted_iota(jnp.int32, (LC, 1), 0).astype(F32)
    hf, hb = _filter_block(pos, 1.0 / LC, w1_ref[...], b1_ref[...], w2_ref[...], b2_ref[...],
                           w3f_ref[...], w3b_ref[...], df_ref[...], db_ref[...])
    fc = fc_ref[...]
    fi = fi_ref[...]
    sf = _dot3(fc, hf)
    sb = _dot3(fc, hb)
    kr = sf[:LC] + sb[:LC]
    ki = sf[LC:] - sb[LC:]
    z = y[:, 0:256]
    fb = fb_ref[...]
    for o in range(HY_ORDER):
        s = _dot3(fc, z)
        sr, si = s[:LC], s[LC:]
        c0 = 256 * o
        yr = sr * kr[:, c0:c0 + 256] - si * ki[:, c0:c0 + 256]
        yi = sr * ki[:, c0:c0 + 256] + si * kr[:, c0:c0 + 256]
        conv = _dot3(fi, jnp.concatenate([yr, yi], axis=0))
        z = y[:, 256 * (o + 1):256 * (o + 2)] * (conv + fb[o:o + 1] * z)
    o_ref[...] = z


def _ctx_hyena(a, conv_w, conv_b, fp, fbias, fc, fi):
    return pl.pallas_call(
        _ctx_hyena_kernel,
        grid=(1,),
        in_specs=[pl.BlockSpec((LC, 768), lambda i: (L // LC, 0)),
                  _const_spec((3, 768)), _const_spec((1, 768)),
                  _const_spec((128, HY_FFN)), _const_spec((1, HY_FFN)), _const_spec((HY_FFN, HY_FFN)),
                  _const_spec((1, HY_FFN)), _const_spec((HY_FFN, 512)), _const_spec((HY_FFN, 512)),
                  _const_spec((1, 512)), _const_spec((1, 512)),
                  _const_spec((HY_ORDER, 256)), _const_spec((2 * LC, LC)), _const_spec((LC, 2 * LC))],
        out_specs=pl.BlockSpec((LC, 256), lambda i: (0, 0)),
        out_shape=jax.ShapeDtypeStruct((LC, 256), F32),
        compiler_params=_params("arbitrary"),
        name="hyena_ctx",
    )(a, conv_w, conv_b, *fp, fbias, fc, fi)


def _s5_in_kernel(x_ref, w_ref, o_ref):
    o_ref[0] = jnp.dot(x_ref[...], w_ref[0], preferred_element_type=F32)


def _s5_in(u8, w3):
    return pl.pallas_call(
        _s5_in_kernel,
        grid=(3, S5_W // S5_TN),
        in_specs=[_const_spec((S5_ROWS, S5_W)),
                  pl.BlockSpec((1, S5_W, S5_TN), lambda m, j: (m, 0, j))],
        out_specs=pl.BlockSpec((1, S5_ROWS, S5_TN), lambda m, j: (m, 0, j)),
        out_shape=jax.ShapeDtypeStruct((3, S5_ROWS, S5_W), F32),
        compiler_params=_params("parallel", "parallel"),
        name="s5_chunk_in",
    )(u8, w3)


def _s5_scan_kernel(f_ref, b_ref, lam_ref, hf_ref, hb_ref, sf, sb, bf, bb):
    j = pl.program_id(0)

    @pl.when(j == 0)
    def _():
        sf[...] = jnp.zeros_like(sf)
        sb[...] = jnp.zeros_like(sb)

    half = S5_W // 2
    lam = lam_ref[...]
    lfr, lfi = lam[0:1, :half], lam[0:1, half:]
    lbr, lbi = lam[1:2, :half], lam[1:2, half:]

    def cmul(lr, li, h):
        hr, hi = h[:, :half], h[:, half:]
        return jnp.concatenate([lr * hr - li * hi, lr * hi + li * hr], axis=1)

    def body(r, carry):
        hf, hb = carry
        bf[pl.ds(r, 1), :] = hf
        hf = cmul(lfr, lfi, hf) + f_ref[0, pl.ds(r, 1), :]
        rb = S5_SB - 1 - r
        bb[pl.ds(rb, 1), :] = hb
        hb = cmul(lbr, lbi, hb) + b_ref[0, pl.ds(rb, 1), :]
        return hf, hb

    hf, hb = lax.fori_loop(0, S5_SB, body, (sf[...], sb[...]))
    sf[...] = hf
    sb[...] = hb
    hf_ref[...] = bf[...].astype(BF16)
    hb_ref[...] = bb[...].astype(BF16)


def _s5_scan(fb, lam8):
    fidx = lambda j: jnp.where(j == 0, S5_NSB - 1, j - 1)
    bidx = lambda j: S5_NSB - 1 - j
    o = jax.ShapeDtypeStruct((S5_ROWS, S5_W), BF16)
    return pl.pallas_call(
        _s5_scan_kernel,
        grid=(S5_NSB,),
        in_specs=[pl.BlockSpec((1, S5_SB, S5_W), lambda j: (1, fidx(j), 0)),
                  pl.BlockSpec((1, S5_SB, S5_W), lambda j: (2, bidx(j), 0)),
                  _const_spec((2, S5_W))],
        out_specs=[pl.BlockSpec((S5_SB, S5_W), lambda j: (fidx(j), 0)),
                   pl.BlockSpec((S5_SB, S5_W), lambda j: (bidx(j), 0))],
        out_shape=[o, o],
        scratch_shapes=[pltpu.VMEM((1, S5_W), F32), pltpu.VMEM((1, S5_W), F32),
                        pltpu.VMEM((S5_SB, S5_W), F32), pltpu.VMEM((S5_SB, S5_W), F32)],
        compiler_params=_params("arbitrary"),
        name="s5_scan",
    )(fb, fb, lam8)


def _s5_out_kernel(y_ref, hf_ref, hb_ref, wf_ref, wb_ref, o_ref):
    o_ref[...] = (y_ref[0] + jnp.dot(hf_ref[...], wf_ref[...], preferred_element_type=F32)
                  + jnp.dot(hb_ref[...], wb_ref[...], preferred_element_type=F32))


def _s5_out(fb, hf, hb, wf, wb):
    wblk = pl.BlockSpec((S5_W, S5_TN), lambda j: (0, j))
    return pl.pallas_call(
        _s5_out_kernel,
        grid=(S5_W // S5_TN,),
        in_specs=[pl.BlockSpec((1, S5_ROWS, S5_TN), lambda j: (0, 0, j)),
                  _const_spec((S5_ROWS, S5_W)), _const_spec((S5_ROWS, S5_W)), wblk, wblk],
        out_specs=pl.BlockSpec((S5_ROWS, S5_TN), lambda j: (0, j)),
        out_shape=jax.ShapeDtypeStruct((S5_ROWS, S5_W), F32),
        compiler_params=_params("parallel"),
        name="s5_chunk_out",
    )(fb, hf, hb, wf, wb)


def _softmax_step(q, k, v, m_sc, l_sc, acc_sc):
    s = lax.dot_general(q, k, (((1,), (1,)), ((), ())), preferred_element_type=F32)
    m_prev = m_sc[...]
    m_new = jnp.maximum(m_prev, jnp.max(s, axis=-1, keepdims=True))
    alpha = jnp.exp(m_prev - m_new)
    p = jnp.exp(s - m_new)
    l_sc[...] = alpha * l_sc[...] + jnp.sum(p, axis=-1, keepdims=True)
    acc_sc[...] = alpha * acc_sc[...] + jnp.dot(p.astype(BF16), v, preferred_element_type=F32)
    m_sc[...] = m_new


def _flash_kernel(q_ref, k_ref, v_ref, kc_ref, vc_ref, o_ref, m_sc, l_sc, acc_sc):
    ki = pl.program_id(2)

    @pl.when(ki == 0)
    def _():
        m_sc[...] = jnp.full_like(m_sc, -1e30)
        l_sc[...] = jnp.zeros_like(l_sc)
        acc_sc[...] = jnp.zeros_like(acc_sc)

    q = q_ref[0].reshape(2 * ATT_TQ, HEAD)
    _softmax_step(q, k_ref[0], v_ref[0], m_sc, l_sc, acc_sc)

    @pl.when(ki == pl.num_programs(2) - 1)
    def _():
        _softmax_step(q, kc_ref[0], vc_ref[0], m_sc, l_sc, acc_sc)
        o_ref[0] = (acc_sc[...] / l_sc[...]).reshape(2, ATT_TQ, HEAD).astype(BF16)


def _flash(q4, k, v):
    ctx = lambda h, qi, ki: (h, L // LC, 0)
    return pl.pallas_call(
        _flash_kernel,
        grid=(KVH, L // ATT_TQ, L // ATT_TK),
        in_specs=[pl.BlockSpec((1, 2, ATT_TQ, HEAD), lambda h, qi, ki: (h, 0, qi, 0)),
                  pl.BlockSpec((1, ATT_TK, HEAD), lambda h, qi, ki: (h, ki, 0)),
                  pl.BlockSpec((1, ATT_TK, HEAD), lambda h, qi, ki: (h, ki, 0)),
                  pl.BlockSpec((1, LC, HEAD), ctx), pl.BlockSpec((1, LC, HEAD), ctx)],
        out_specs=pl.BlockSpec((1, 2, ATT_TQ, HEAD), lambda h, qi, ki: (h, 0, qi, 0)),
        out_shape=jax.ShapeDtypeStruct((KVH, 2, L, HEAD), BF16),
        scratch_shapes=[pltpu.VMEM((2 * ATT_TQ, 1), F32), pltpu.VMEM((2 * ATT_TQ, 1), F32),
                        pltpu.VMEM((2 * ATT_TQ, HEAD), F32)],
        compiler_params=_params("parallel", "parallel", "arbitrary"),
        name="flash_attention",
    )(q4, k, v, k, v)


def _ctx_attn_kernel(q_ref, k_ref, v_ref, o_ref):
    q = q_ref[0].reshape(2 * LC, HEAD)
    s = lax.dot_general(q, k_ref[0], (((1,), (1,)), ((), ())), preferred_element_type=F32)
    p = jnp.exp(s - jnp.max(s, axis=-1, keepdims=True))
    o = jnp.dot(p.astype(BF16), v_ref[0], preferred_element_type=F32) / jnp.sum(p, axis=-1, keepdims=True)
    o_ref[0] = o.reshape(2, LC, HEAD).astype(BF16)


def _ctx_attn(q4, k, v):
    ctx = lambda h: (h, L // LC, 0)
    return pl.pallas_call(
        _ctx_attn_kernel,
        grid=(KVH,),
        in_specs=[pl.BlockSpec((1, 2, LC, HEAD), lambda h: (h, 0, L // LC, 0)),
                  pl.BlockSpec((1, LC, HEAD), ctx), pl.BlockSpec((1, LC, HEAD), ctx)],
        out_specs=pl.BlockSpec((1, 2, LC, HEAD), lambda h: (h, 0, 0, 0)),
        out_shape=jax.ShapeDtypeStruct((KVH, 2, LC, HEAD), BF16),
        compiler_params=_params("parallel"),
        name="ctx_attention",
    )(q4, k, v)


def _gelu_tanh(x):
    return 0.5 * x * (1.0 + jnp.tanh(math.sqrt(2.0 / math.pi) * (x + 0.044715 * (x * x * x))))


def _post_kernel(x_ref, mod_ref, hyl_ref, hyc_ref, s5_ref, po_ref, atl_ref, atc_ref,
                 gw_ref, gb_ref, wo_ref, g1_ref, g2_ref, g3_ref, w1_ref, w2_ref, o_ref):
    is_ctx = pl.program_id(0) >= NLB
    m = mod_ref[0]
    g = _gelu_tanh(s5_ref[...])
    s5o = g * jax.nn.sigmoid(_bdot(g, gw_ref[...]) + gb_ref[...])
    hy = jnp.where(is_ctx, hyc_ref[...], hyl_ref[...])
    o = (_bdot(hy, wo_ref[0:256, :]) + _bdot(s5o, wo_ref[256:512, :]) + _bdot(po_ref[...], wo_ref[512:768, :]))
    for hh in range(QH):
        at = jnp.where(is_ctx, atc_ref[hh // 2, hh % 2], atl_ref[hh // 2, hh % 2])
        r0 = 768 + HEAD * hh
        o = o + jnp.dot(at, wo_ref[r0:r0 + HEAD, :], preferred_element_type=F32)
    x = x_ref[...] + m[2:3] * _rms(o, g1_ref[...])
    h = _rms(x, g2_ref[...]) * (1.0 + m[4:5]) + m[3:4]
    f = jnp.dot(h.astype(BF16), w1_ref[...], preferred_element_type=F32)
    f = jnp.square(jnp.maximum(f, 0.0)).astype(BF16)
    f = jnp.dot(f, w2_ref[...], preferred_element_type=F32)
    o_ref[...] = x + m[5:6] * _rms(f, g3_ref[...])


def _post(x, mods, hy_l, hy_c, s5y, po, at_l, at_c, glu_w, glu_b, w_out, g1, g2, g3, w1, w2):
    row = lambda w: pl.BlockSpec((TR, w), lambda i: (i, 0))
    lat = lambda i: jnp.minimum(i, NLB - 1)
    return pl.pallas_call(
        _post_kernel,
        grid=(NRB,),
        in_specs=[row(D),
                  pl.BlockSpec((1, N_MOD, D), lambda i: (i // NLB, 0, 0)),
                  pl.BlockSpec((TR, 256), lambda i: (lat(i), 0)),
                  _const_spec((LC, 256)),
                  row(256), row(256),
                  pl.BlockSpec((KVH, 2, TR, HEAD), lambda i: (0, 0, lat(i), 0)),
                  _const_spec((KVH, 2, LC, HEAD)),
                  _const_spec((256, 256)), _const_spec((1, 256)), _const_spec((D, D)),
                  _const_spec((1, D)), _const_spec((1, D)), _const_spec((1, D)),
                  _const_spec((D, D_FF)), _const_spec((D_FF, D))],
        out_specs=row(D),
        out_shape=jax.ShapeDtypeStruct((LT, D), F32),
        compiler_params=_params("parallel"),
        name="post_mix_mlp",
    )(x, mods, hy_l, hy_c, s5y, po, at_l, at_c, glu_w, glu_b, w_out, g1, g2, g3, w1, w2)


def _dft_tables():
    n = jnp.arange(128, dtype=jnp.int32)
    m = ((2 * n[:, None] + 1) * n[None, :]) % (2 * FFT_N1)
    ang = m.astype(F32) * (2.0 * math.pi / (2 * FFT_N1))
    c1, s1 = jnp.cos(ang), jnp.sin(ang)
    m1 = jnp.concatenate([c1, -s1], axis=0)
    m1inv = jnp.concatenate([c1.T, -s1.T], axis=1) * (2.0 / FFT_N)
    kk = n[:, None, None] + FFT_N1 * n[None, :, None]
    mm = ((2 * kk + 1) * n[None, None, :]) % (2 * FFT_N)
    phi = mm.astype(F32) * (2.0 * math.pi / (2 * FFT_N))
    cm, sm = jnp.cos(phi), jnp.sin(phi)
    w = jnp.concatenate([jnp.concatenate([cm, sm], axis=2), jnp.concatenate([-sm, cm], axis=2)], axis=1)
    wt = jnp.swapaxes(w, 1, 2)
    nc = jnp.arange(LC, dtype=jnp.int32)
    mc = ((2 * nc[:, None] + 1) * nc[None, :]) % (4 * LC)
    angc = mc.astype(F32) * (2.0 * math.pi / (4 * LC))
    cc, sc = jnp.cos(angc), jnp.sin(angc)
    fc = jnp.concatenate([cc, -sc], axis=0)
    fi = jnp.concatenate([cc.T, -sc.T], axis=1) * (2.0 / (2 * LC))
    return m1.astype(BF16), m1inv.astype(BF16), w.astype(BF16), wt.astype(BF16), fc, fi


def _rope_tables():
    t = jnp.arange(L, dtype=jnp.int32)
    inv = ROPE_THETA ** (-jnp.arange(0, HEAD // 2, 2, dtype=F32) / (HEAD // 2))
    ang = jnp.concatenate([(t // GRID_W).astype(F32)[:, None] * inv[None, :],
                           (t % GRID_W).astype(F32)[:, None] * inv[None, :]], axis=-1)
    ang = jnp.concatenate([ang, jnp.zeros((LC, HEAD // 2), F32)], axis=0)
    c, s = jnp.cos(ang), jnp.sin(ang)
    return jnp.concatenate([c, c, c, c], axis=1), jnp.concatenate([-s, s, -s, s], axis=1)


def _s5_tables(a_re, a_im, log_dt, b_re, b_im, c_re, c_im, d):
    dt = jnp.exp(log_dt)[..., None]
    tau = jnp.arange(S5_T + 1, dtype=F32)[:, None, None, None]
    mag = jnp.exp(a_re * dt * tau)
    pr, pi = mag * jnp.cos(a_im * dt * tau), mag * jnp.sin(a_im * dt * tau)
    lam_re, lam_im = pr[1], pi[1]
    den = a_re * a_re + a_im * a_im
    nr, ni = lam_re - 1.0, lam_im
    cr = (nr * a_re + ni * a_im) / den
    ci = (ni * a_re - nr * a_im) / den
    bb_re = cr[..., None] * b_re - ci[..., None] * b_im
    bb_im = cr[..., None] * b_im + ci[..., None] * b_re
    cl_re = c_re[None] * pr[:, :, :, None, :] - c_im[None] * pi[:, :, :, None, :]
    cl_im = c_re[None] * pi[:, :, :, None, :] + c_im[None] * pr[:, :, :, None, :]
    kk = jnp.einsum('tdghp,dgpk->tdghk', cl_re, bb_re) - jnp.einsum('tdghp,dgpk->tdghk', cl_im, bb_im)
    eye_g = jnp.eye(S5_G, dtype=F32)
    k0 = kk[0, 0] + kk[0, 1] + d.reshape(S5_G, S5_H)[:, :, None] * jnp.eye(S5_H, dtype=F32)[None]
    kfull = jnp.concatenate([kk[1:S5_T, 1][::-1], k0[None], kk[1:S5_T, 0]], axis=0)
    st = jnp.arange(S5_T)
    kt = kfull[st[None, :] - st[:, None] + S5_T - 1]
    m_intra = jnp.einsum('stghk,gj->sgktjh', kt, eye_g).reshape(S5_W, S5_W)
    def state_in(pw_re, pw_im, d_):
        re = pw_re[:, :, :, None] * bb_re[d_][None] - pw_im[:, :, :, None] * bb_im[d_][None]
        im = pw_re[:, :, :, None] * bb_im[d_][None] + pw_im[:, :, :, None] * bb_re[d_][None]
        both = jnp.stack([re, im], axis=0)
        return jnp.einsum('rsgpk,gj->sgkrjp', both, eye_g).reshape(S5_W, S5_W)
    m_fst = state_in(pr[:S5_T, 0][::-1], pi[:S5_T, 0][::-1], 0)
    m_bst = state_in(pr[:S5_T, 1], pi[:S5_T, 1], 1)
    def state_out(cre, cim):
        both = jnp.stack([cre, -cim], axis=0)
        return jnp.einsum('rtghp,gj->rgptjh', both, eye_g).reshape(S5_W, S5_W)
    m_fout = state_out(cl_re[1:, 0], cl_im[1:, 0])
    m_bout = state_out(cl_re[1:, 1][::-1], cl_im[1:, 1][::-1])
    lam8 = jnp.concatenate([pr[S5_T].reshape(2, -1), pi[S5_T].reshape(2, -1)], axis=1)
    w3 = jnp.stack([m_intra, m_fst, m_bst], axis=0).astype(BF16)
    return w3, lam8, m_fout.astype(BF16), m_bout.astype(BF16)


def _filter_params(w1, b1, w2, b2, w3, decay):
    w1p = jnp.concatenate([w1, jnp.zeros((128 - w1.shape[0], HY_FFN), F32)], axis=0)
    w3r = w3.reshape(HY_FFN, HY_ORDER, 2, DG)
    return (w1p, b1.reshape(1, -1), w2, b2.reshape(1, -1),
            w3r[:, :, 0].reshape(HY_FFN, 512), w3r[:, :, 1].reshape(HY_FFN, 512),
            decay[:, 0].reshape(1, 512), decay[:, 1].reshape(1, 512))


def _block_diag(w):
    g, n, _ = w.shape
    return jnp.einsum('gcd,gj->gcjd', w, jnp.eye(g, dtype=w.dtype)).reshape(g * n, g * n)


def kernel(x, c, ctx, c_ctx, mod_w, mod_b, norm_pre_mix, norm_post_mix, norm_pre_mlp, norm_post_mlp,
           w_in, w_out, hy_conv_w, hy_conv_b, hy_ffn_w1, hy_ffn_b1, hy_ffn_w2, hy_ffn_b2, hy_ffn_w3,
           hy_decay, hy_bias, s5_a_re, s5_a_im, s5_log_dt, s5_b_re, s5_b_im, s5_c_re, s5_c_im, s5_d,
           s5_glu_w, s5_glu_b, pool_w, pool_scale, att_q_norm, att_k_norm, mlp_w1, mlp_w2):
    xs = jnp.concatenate([x[0], ctx[0]], axis=0)
    mods = _modulation(c, c_ctx, mod_w, mod_b)
    m1, m1inv, wk, wkt, fc, fi = _dft_tables()
    cs, sn = _rope_tables()
    perm = jnp.concatenate([jnp.arange(0, HEAD, 2), jnp.arange(1, HEAD, 2)])
    qcols = 1280 + (jnp.arange(QH)[:, None] * HEAD + perm[None, :]).reshape(-1)
    kcols = 1536 + (jnp.arange(KVH)[:, None] * HEAD + perm[None, :]).reshape(-1)
    cols = jnp.concatenate([jnp.arange(1280), qcols, kcols, jnp.arange(1664, IN_COLS)])
    head_mean = _block_diag(jnp.full((QH, HEAD, HEAD), 1.0 / HEAD, F32)).astype(BF16)

    for i in range(DEPTH):
        w_in_i = w_in[i][:, cols].astype(BF16)
        qg = jnp.tile(att_q_norm[i][perm], QH).reshape(1, 256)
        kg = jnp.tile(att_k_norm[i][perm], KVH).reshape(1, 128)
        a, s, p, q, k, v = _premix(xs, mods[i], norm_pre_mix[i].reshape(1, D), w_in_i, cs, sn, qg, kg, head_mean)

        vv, x1, x2, po = _local(a, p, hy_conv_w[i], hy_conv_b[i].reshape(1, 768),
                                _block_diag(pool_w[i]), pool_scale[i].reshape(1, 256))

        fp = _filter_params(hy_ffn_w1[i], hy_ffn_b1[i], hy_ffn_w2[i], hy_ffn_b2[i], hy_ffn_w3[i], hy_decay[i])
        filt = _filter_gen(fp)
        fa = _fft_fwd1(filt.reshape(FFT_R, FFT_N2 * 1024), m1)
        kf = _filt_spec(fa.reshape(2, FFT_K1, FFT_N2, 1024), wk)
        z = vv.reshape(LT // FFT_N2, FFT_N2 * 256)
        gates = (x1.reshape(LT // FFT_N2, FFT_N2 * 256), x2.reshape(LT // FFT_N2, FFT_N2 * 256))
        for o in range(HY_ORDER):
            fa_z = _fft_fwd1(z, m1)
            bm = _fft_mid(fa_z.reshape(2, FFT_K1, FFT_N2, 256), wk, wkt, kf, o)
            z = _fft_inv1(bm.reshape(FFT_N1, FFT_N2 * 256), m1inv, gates[o], z,
                          jnp.tile(hy_bias[i][o], FFT_TC // 256).reshape(1, FFT_TC))
        hy_l = z.reshape(L, 256)
        hy_c = _ctx_hyena(a, hy_conv_w[i], hy_conv_b[i].reshape(1, 768), fp, hy_bias[i], fc, fi)

        w3, lam8, m_fout, m_bout = _s5_tables(s5_a_re[i], s5_a_im[i], s5_log_dt[i], s5_b_re[i], s5_b_im[i],
                                              s5_c_re[i], s5_c_im[i], s5_d[i])
        fb = _s5_in(s.reshape(S5_ROWS, S5_W), w3)
        hf, hb = _s5_scan(fb, lam8)
        s5y = _s5_out(fb, hf, hb, m_fout, m_bout).reshape(LT, 256)

        q4 = q.reshape(KVH, 2, LT, HEAD)
        at_l = _flash(q4, k, v)
        at_c = _ctx_attn(q4, k, v)

        xs = _post(xs, mods[i], hy_l, hy_c, s5y, po, at_l, at_c,
                   s5_glu_w[i].astype(BF16), s5_glu_b[i].reshape(1, 256), w_out[i].astype(BF16),
                   norm_post_mix[i].reshape(1, D), norm_pre_mlp[i].reshape(1, D), norm_post_mlp[i].reshape(1, D),
                   mlp_w1[i].astype(BF16), mlp_w2[i].astype(BF16))
    return xs[:L][None]
```

```python
import functools
import math

import jax
import jax.numpy as jnp
from jax import lax
from jax.experimental import pallas as pl
from jax.experimental.pallas import tpu as pltpu

F32 = jnp.float32
BF16 = jnp.bfloat16

D = 1024
L = 16384
LC = 256
LT = L + LC
DEPTH = 4
GRID_W = 64
EPS = 1e-6
N_MOD = 6
DG = 256
HY_ORDER = 2
HY_FREQS = 16
HY_FFN = 64
HY_SIN_FREQ = 1.0
S5_H = 16
S5_G = 16
S5_P = 64
HEAD = 64
QH = 4
KVH = 2
ATT_SCALE = 1.0 / math.sqrt(HEAD)
ROPE_THETA = 10000.0
D_FF = 4 * D
IN_COLS = 1792

TR = 256
NLB = L // TR
NRB = LT // TR
HALO = 8

FFT_N = 2 * L
FFT_N1 = 256
FFT_N2 = 128
FFT_K1 = FFT_N1 // 2
FFT_R = L // FFT_N2
FFT_TJ = 8
FFT_TK = 8

S5_T = 8
S5_W = S5_T * DG
S5_ROWS = LT // S5_T
S5_SB = LC // S5_T
S5_NSB = S5_ROWS // S5_SB
S5_TN = 512

ATT_TQ = 1024
ATT_TK = 1024
ATT_GQ = 256
ATT_NG = 2 * ATT_TQ // ATT_GQ
ATT_VE = HEAD + 16
LOG2E = math.log2(math.e)

VMEM_LIMIT = 56 * 1024 * 1024


def _params(*sem):
    return pltpu.CompilerParams(dimension_semantics=sem, vmem_limit_bytes=VMEM_LIMIT)


def _const_spec(shape):
    nd = len(shape)
    return pl.BlockSpec(shape, lambda *_: (0,) * nd, pipeline_mode=pl.Buffered(1))


def _bdot(a, b):
    return jnp.dot(a.astype(BF16), b.astype(BF16), preferred_element_type=F32)


def _split(a):
    hi = a.astype(BF16)
    lo = (a - hi.astype(F32)).astype(BF16)
    return hi, lo


def _dot3(a, b):
    ah, al = _split(a)
    bh, bl = _split(b)
    return (jnp.dot(ah, bh, preferred_element_type=F32)
            + jnp.dot(al, bh, preferred_element_type=F32)
            + jnp.dot(ah, bl, preferred_element_type=F32))


def _rms(x, g):
    return x * lax.rsqrt(jnp.mean(x * x, axis=-1, keepdims=True) + EPS) * g


def _mod_kernel(c_ref, w_ref, b_ref, o_ref):
    c = c_ref[...]
    s = c * jax.nn.sigmoid(c)
    o_ref[0] = _dot3(s, w_ref[0]) + b_ref[0]


def _modulation(c, c_ctx, mod_w, mod_b):
    cc = jnp.concatenate([c.reshape(1, D), c_ctx.reshape(1, D), jnp.zeros((6, D), F32)], axis=0)
    tn = 1536
    out = pl.pallas_call(
        _mod_kernel,
        grid=(DEPTH, N_MOD * D // tn),
        in_specs=[pl.BlockSpec((8, D), lambda i, j: (0, 0)),
                  pl.BlockSpec((1, D, tn), lambda i, j: (i, 0, j)),
                  pl.BlockSpec((1, 1, tn), lambda i, j: (i, 0, j))],
        out_specs=pl.BlockSpec((1, 8, tn), lambda i, j: (i, 0, j)),
        out_shape=jax.ShapeDtypeStruct((DEPTH, 8, N_MOD * D), F32),
        compiler_params=_params("parallel", "parallel"),
        name="modulation",
    )(cc, mod_w, mod_b.reshape(DEPTH, 1, N_MOD * D))
    return out[:, :2].reshape(DEPTH, 2, N_MOD, D)


def _qknorm_rope(t, gain, cs, sn, hm):
    hi, lo = _split(t * t)
    ms = jnp.dot(hi, hm, preferred_element_type=F32) + jnp.dot(lo, hm, preferred_element_type=F32)
    tn = t * lax.rsqrt(ms + EPS) * gain
    w = t.shape[1]
    lane = lax.broadcasted_iota(jnp.int32, tn.shape, 1)
    swapped = jnp.where((lane % HEAD) < HEAD // 2, pltpu.roll(tn, w - HEAD // 2, 1), pltpu.roll(tn, HEAD // 2, 1))
    return tn * cs + swapped * sn


def _premix_kernel(x_ref, mod_ref, g_ref, w_ref, cs_ref, sn_ref, qg_ref, kg_ref, hm_ref,
                   a_ref, s_ref, p_ref, q_ref, k_ref, v_ref):
    m = mod_ref[0]
    h = _rms(x_ref[...], g_ref[...]) * (1.0 + m[1:2]) + m[0:1]
    u = jnp.dot(h.astype(BF16), w_ref[...], preferred_element_type=F32)
    a_ref[...] = u[:, 0:768]
    s_ref[...] = u[:, 768:1024].astype(BF16)
    p_ref[...] = u[:, 1024:1280]
    cs = cs_ref[...]
    sn = sn_ref[...]
    hm = hm_ref[...]
    q = _qknorm_rope(u[:, 1280:1536], qg_ref[...], jnp.concatenate([cs, cs], axis=1),
                     jnp.concatenate([sn, sn], axis=1), hm) * (ATT_SCALE * LOG2E)
    k = _qknorm_rope(u[:, 1536:1664], kg_ref[...], cs, sn, hm[:128, :128])
    q_ref[...] = q.T.astype(BF16)
    v_ref[...] = u[:, 1664:1792].T.astype(BF16)
    for hh in range(KVH):
        k_ref[hh] = k[:, HEAD * hh:HEAD * (hh + 1)].astype(BF16)


def _premix(x, mods, g, w_in, cs, sn, qg, kg, hm):
    row = lambda w: pl.BlockSpec((TR, w), lambda i: (i, 0))
    return pl.pallas_call(
        _premix_kernel,
        grid=(NRB,),
        in_specs=[row(D),
                  pl.BlockSpec((1, N_MOD, D), lambda i: (i // NLB, 0, 0)),
                  _const_spec((1, D)), _const_spec((D, IN_COLS)),
                  row(128), row(128),
                  _const_spec((1, 256)), _const_spec((1, 128)), _const_spec((256, 256))],
        out_specs=[row(768), row(256), row(256),
                   pl.BlockSpec((QH * HEAD, TR), lambda i: (0, i)),
                   pl.BlockSpec((KVH, TR, HEAD), lambda i: (0, i, 0)),
                   pl.BlockSpec((KVH * HEAD, TR), lambda i: (0, i))],
        out_shape=[jax.ShapeDtypeStruct((LT, 768), F32),
                   jax.ShapeDtypeStruct((LT, 256), BF16),
                   jax.ShapeDtypeStruct((LT, 256), F32),
                   jax.ShapeDtypeStruct((QH * HEAD, LT), BF16),
                   jax.ShapeDtypeStruct((KVH, LT, HEAD), BF16),
                   jax.ShapeDtypeStruct((KVH * HEAD, LT), BF16)],
        compiler_params=_params("parallel"),
        name="premix",
    )(x, mods, g, w_in, cs, sn, qg, kg, hm)


def _local_kernel(a_ref, ap_ref, an_ref, p_ref, pp_ref, pn_ref, cw_ref, cb_ref, pw_ref, ps_ref,
                  v_ref, x1_ref, x2_ref, po_ref):
    i = pl.program_id(0)
    is_ctx = i >= NLB
    s0 = jnp.where(is_ctx, L, 0)
    s1 = jnp.where(is_ctx, LT, L)
    n = TR + 2 * HALO
    idx = i * TR - HALO + lax.broadcasted_iota(jnp.int32, (n, 1), 0)
    valid = (idx >= s0) & (idx < s1)

    def shifted(e, d):
        return pltpu.roll(e, (-d) % n, 0)[HALO:HALO + TR] if d else e[HALO:HALO + TR]

    ea = jnp.where(valid, jnp.concatenate([ap_ref[...], a_ref[...], an_ref[...]], axis=0), 0.0)
    w = cw_ref[...]
    y = cb_ref[...] + w[0:1] * shifted(ea, -1) + w[1:2] * shifted(ea, 0) + w[2:3] * shifted(ea, 1)
    v_ref[...] = y[:, 0:256]
    x1_ref[...] = y[:, 256:512]
    x2_ref[...] = y[:, 512:768]

    e = jnp.where(valid, jnp.concatenate([pp_ref[...], p_ref[...], pn_ref[...]], axis=0), 0.0)
    w2 = e + pltpu.roll(e, 1, 0)
    w4 = pltpu.roll(w2, 1, 0) + pltpu.roll(w2, n - 1, 0)
    w8 = pltpu.roll(w4, 2, 0) + pltpu.roll(w4, n - 2, 0)
    w16 = pltpu.roll(w8, 4, 0) + pltpu.roll(w8, n - 4, 0)
    lane = lax.broadcasted_iota(jnp.int32, (TR, 256), 1)
    cut = lambda t: t[HALO:HALO + TR]
    sums = jnp.where(lane < 64, cut(w2), jnp.where(lane < 128, cut(w4), jnp.where(lane < 192, cut(w8), cut(w16))))
    half = jnp.where(lane < 64, 1, jnp.where(lane < 128, 2, jnp.where(lane < 192, 4, 8)))
    tl = i * TR + lax.broadcasted_iota(jnp.int32, (TR, 256), 0) - s0
    cnt = jnp.minimum(tl + half, s1 - s0) - jnp.maximum(tl - half, 0)
    pooled = sums / cnt.astype(F32) - p_ref[...]
    po_ref[...] = _bdot(pooled, pw_ref[...]) * ps_ref[...]


def _local(a, p, conv_w, conv_b, pool_bd, pool_scale):
    r8 = TR // HALO
    main = lambda w: pl.BlockSpec((TR, w), lambda i: (i, 0))
    prev = lambda w: pl.BlockSpec((HALO, w), lambda i: (jnp.maximum(i * r8 - 1, 0), 0))
    nxt = lambda w: pl.BlockSpec((HALO, w), lambda i: (jnp.minimum((i + 1) * r8, LT // HALO - 1), 0))
    o = jax.ShapeDtypeStruct((LT, 256), F32)
    return pl.pallas_call(
        _local_kernel,
        grid=(NRB,),
        in_specs=[main(768), prev(768), nxt(768), main(256), prev(256), nxt(256),
                  _const_spec((3, 768)), _const_spec((1, 768)), _const_spec((256, 256)), _const_spec((1, 256))],
        out_specs=[main(256)] * 4,
        out_shape=[o, o, o, o],
        compiler_params=_params("parallel"),
        name="local_ops",
    )(a, a, a, p, p, p, conv_w, conv_b, pool_bd, pool_scale)


def _filter_hidden(n, inv_len, w1, b1, w2, b2):
    t = n * inv_len
    lane = lax.broadcasted_iota(jnp.int32, (n.shape[0], 128), 1)
    fr = jnp.where(lane <= HY_FREQS, lane, lane - HY_FREQS).astype(F32)
    ang = (2.0 * math.pi * t) * fr
    feats = jnp.where(lane == 0, t,
                      jnp.where(lane <= HY_FREQS, jnp.cos(ang),
                                jnp.where(lane <= 2 * HY_FREQS, jnp.sin(ang), 0.0)))
    h = jnp.sin(HY_SIN_FREQ * (_dot3(feats, w1) + b1))
    h = jnp.sin(HY_SIN_FREQ * (_dot3(h, w2) + b2))
    return h, t


def _filter_block(n, inv_len, w1, b1, w2, b2, w3f, w3b, dec_f, dec_b):
    h, t = _filter_hidden(n, inv_len, w1, b1, w2, b2)
    hf = _dot3(h, w3f) * jnp.exp(-t * jnp.abs(dec_f))
    hp, tp = _filter_hidden(n - 1.0, inv_len, w1, b1, w2, b2)
    hb = _dot3(hp, w3b) * jnp.exp(-tp * jnp.abs(dec_b))
    hb = jnp.where(n >= 1.0, hb, 0.0)
    return hf, hb


def _filter_kernel(w1_ref, b1_ref, w2_ref, b2_ref, w3f_ref, w3b_ref, df_ref, db_ref, o_ref, *, rows):
    i = pl.program_id(0)
    n = (i * rows + lax.broadcasted_iota(jnp.int32, (rows, 1), 0)).astype(F32)
    hf, hb = _filter_block(n, 1.0 / L, w1_ref[...], b1_ref[...], w2_ref[...], b2_ref[...],
                           w3f_ref[...], w3b_ref[...], df_ref[...], db_ref[...])
    o_ref[:, 0:256] = hf[:, 0:256]
    o_ref[:, 256:512] = hb[:, 0:256]
    o_ref[:, 512:768] = hf[:, 256:512]
    o_ref[:, 768:1024] = hb[:, 256:512]


def _filter_gen(fp):
    rows = 512
    return pl.pallas_call(
        functools.partial(_filter_kernel, rows=rows),
        grid=(L // rows,),
        in_specs=[_const_spec((128, HY_FFN)), _const_spec((1, HY_FFN)), _const_spec((HY_FFN, HY_FFN)),
                  _const_spec((1, HY_FFN)), _const_spec((HY_FFN, 512)), _const_spec((HY_FFN, 512)),
                  _const_spec((1, 512)), _const_spec((1, 512))],
        out_specs=pl.BlockSpec((rows, 1024), lambda i: (i, 0)),
        out_shape=jax.ShapeDtypeStruct((L, 1024), F32),
        compiler_params=_params("parallel"),
        name="hyena_filter",
    )(*fp)


def _fft_fwd1_kernel(x_ref, m_ref, o_ref):
    for jj in range(FFT_TJ):
        res = jnp.dot(m_ref[...], x_ref[:, jj, :].astype(BF16), preferred_element_type=F32)
        o_ref[0, :, jj, :] = res[:FFT_K1]
        o_ref[1, :, jj, :] = res[FFT_K1:]


def _fft_fwd1(x3, m1):
    c = x3.shape[2]
    return pl.pallas_call(
        _fft_fwd1_kernel,
        grid=(FFT_N2 // FFT_TJ, c // 256),
        in_specs=[pl.BlockSpec((FFT_R, FFT_TJ, 256), lambda j, cc: (0, j, cc)), _const_spec((FFT_N1, FFT_R))],
        out_specs=pl.BlockSpec((2, FFT_K1, FFT_TJ, 256), lambda j, cc: (0, 0, j, cc)),
        out_shape=jax.ShapeDtypeStruct((2, FFT_K1, FFT_N2, c), F32),
        compiler_params=_params("parallel", "parallel"),
        name="fft_fwd1",
    )(x3, m1)


def _filt_spec_kernel(a_ref, w_ref, o_ref):
    for j in range(FFT_TK):
        a = jnp.concatenate([a_ref[0, j], a_ref[1, j]], axis=0).astype(BF16)
        x = jnp.dot(w_ref[j], a, preferred_element_type=F32)
        o_ref[0, 0, j] = x[:FFT_N2, 0:256] + x[:FFT_N2, 256:512]
        o_ref[0, 1, j] = x[FFT_N2:, 0:256] - x[FFT_N2:, 256:512]


def _filt_spec(fa, w):
    return pl.pallas_call(
        _filt_spec_kernel,
        grid=(HY_ORDER, FFT_K1 // FFT_TK),
        in_specs=[pl.BlockSpec((2, FFT_TK, FFT_N2, 512), lambda o, i: (0, i, 0, o)),
                  pl.BlockSpec((FFT_TK, 256, 256), lambda o, i: (i, 0, 0))],
        out_specs=pl.BlockSpec((1, 2, FFT_TK, FFT_N2, 256), lambda o, i: (o, 0, i, 0, 0)),
        out_shape=jax.ShapeDtypeStruct((HY_ORDER, 2, FFT_K1, FFT_N2, 256), F32),
        compiler_params=_params("parallel", "parallel"),
        name="hyena_filter_spectrum",
    )(fa, w)


def _fft_mid_kernel(a_ref, w_ref, wt_ref, k_ref, o_ref):
    for j in range(FFT_TK):
        a = jnp.concatenate([a_ref[0, j], a_ref[1, j]], axis=0).astype(BF16)
        x = jnp.dot(w_ref[j], a, preferred_element_type=F32)
        xr, xi = x[:FFT_N2], x[FFT_N2:]
        kr, ki = k_ref[0, 0, j], k_ref[0, 1, j]
        y = jnp.concatenate([xr * kr - xi * ki, xr * ki + xi * kr], axis=0).astype(BF16)
        b = jnp.dot(wt_ref[j], y, preferred_element_type=F32)
        o_ref[0, j] = b[:FFT_N2]
        o_ref[1, j] = b[FFT_N2:]


def _fft_mid(a, w, wt, kf, order):
    blk = pl.BlockSpec((2, FFT_TK, FFT_N2, 256), lambda i: (0, i, 0, 0))
    wblk = pl.BlockSpec((FFT_TK, 256, 256), lambda i: (i, 0, 0))
    return pl.pallas_call(
        _fft_mid_kernel,
        grid=(FFT_K1 // FFT_TK,),
        in_specs=[blk, wblk, wblk,
                  pl.BlockSpec((1, 2, FFT_TK, FFT_N2, 256), lambda i: (order, 0, i, 0, 0))],
        out_specs=blk,
        out_shape=jax.ShapeDtypeStruct((2, FFT_K1, FFT_N2, 256), F32),
        compiler_params=_params("parallel"),
        name="fft_mid",
    )(a, w, wt, kf)


def _fft_inv1_kernel(b_ref, m_ref, x_ref, z_ref, fb_ref, o_ref):
    for jj in range(FFT_TJ):
        b = jnp.concatenate([b_ref[0, :, jj, :], b_ref[1, :, jj, :]], axis=0).astype(BF16)
        y = jnp.dot(m_ref[...], b, preferred_element_type=F32)
        o_ref[:, jj, :] = x_ref[:, jj, :] * (y + fb_ref[...] * z_ref[:, jj, :])


def _fft_inv1(b4, minv, gate3, z3, fb):
    sig = pl.BlockSpec((FFT_R, FFT_TJ, 256), lambda j: (0, j, 0))
    return pl.pallas_call(
        _fft_inv1_kernel,
        grid=(FFT_N2 // FFT_TJ,),
        in_specs=[pl.BlockSpec((2, FFT_K1, FFT_TJ, 256), lambda j: (0, 0, j, 0)), _const_spec((FFT_R, FFT_N1)),
                  sig, sig, _const_spec((1, 256))],
        out_specs=sig,
        out_shape=jax.ShapeDtypeStruct((FFT_R, FFT_N2, 256), F32),
        compiler_params=_params("parallel"),
        name="fft_inv1_gate",
    )(b4, minv, gate3, z3, fb)


def _ctx_hyena_kernel(a_ref, cw_ref, cb_ref, w1_ref, b1_ref, w2_ref, b2_ref, w3f_ref, w3b_ref, df_ref, db_ref,
                      fb_ref, fc_ref, fi_ref, o_ref):
    zero = jnp.zeros((HALO, 768), F32)
    n = LC + 2 * HALO
    ea = jnp.concatenate([zero, a_ref[...], zero], axis=0)
    w = cw_ref[...]
    y = (cb_ref[...] + w[0:1] * pltpu.roll(ea, 1, 0)[HALO:HALO + LC] + w[1:2] * ea[HALO:HALO + LC]
         + w[2:3] * pltpu.roll(ea, n - 1, 0)[HALO:HALO + LC])
    pos = lax.broadcasted_iota(jnp.int32, (LC, 1), 0).astype(F32)
    hf, hb = _filter_block(pos, 1.0 / LC, w1_ref[...], b1_ref[...], w2_ref[...], b2_ref[...],
                           w3f_ref[...], w3b_ref[...], df_ref[...], db_ref[...])
    fc = fc_ref[...]
    fi = fi_ref[...]
    sf = _dot3(fc, hf)
    sb = _dot3(fc, hb)
    kr = sf[:LC] + sb[:LC]
    ki = sf[LC:] - sb[LC:]
    z = y[:, 0:256]
    fb = fb_ref[...]
    for o in range(HY_ORDER):
        s = _dot3(fc, z)
        sr, si = s[:LC], s[LC:]
        c0 = 256 * o
        yr = sr * kr[:, c0:c0 + 256] - si * ki[:, c0:c0 + 256]
        yi = sr * ki[:, c0:c0 + 256] + si * kr[:, c0:c0 + 256]
        conv = _dot3(fi, jnp.concatenate([yr, yi], axis=0))
        z = y[:, 256 * (o + 1):256 * (o + 2)] * (conv + fb[o:o + 1] * z)
    o_ref[...] = z


def _ctx_hyena(a, conv_w, conv_b, fp, fbias, fc, fi):
    return pl.pallas_call(
        _ctx_hyena_kernel,
        grid=(1,),
        in_specs=[pl.BlockSpec((LC, 768), lambda i: (L // LC, 0)),
                  _const_spec((3, 768)), _const_spec((1, 768)),
                  _const_spec((128, HY_FFN)), _const_spec((1, HY_FFN)), _const_spec((HY_FFN, HY_FFN)),
                  _const_spec((1, HY_FFN)), _const_spec((HY_FFN, 512)), _const_spec((HY_FFN, 512)),
                  _const_spec((1, 512)), _const_spec((1, 512)),
                  _const_spec((HY_ORDER, 256)), _const_spec((2 * LC, LC)), _const_spec((LC, 2 * LC))],
        out_specs=pl.BlockSpec((LC, 256), lambda i: (0, 0)),
        out_shape=jax.ShapeDtypeStruct((LC, 256), F32),
        compiler_params=_params("arbitrary"),
        name="hyena_ctx",
    )(a, conv_w, conv_b, *fp, fbias, fc, fi)


def _s5_in_kernel(x_ref, w_ref, o_ref):
    o_ref[0] = jnp.dot(x_ref[...], w_ref[0], preferred_element_type=F32)


def _s5_in(u8, w3):
    return pl.pallas_call(
        _s5_in_kernel,
        grid=(3, S5_W // S5_TN),
        in_specs=[_const_spec((S5_ROWS, S5_W)),
                  pl.BlockSpec((1, S5_W, S5_TN), lambda m, j: (m, 0, j))],
        out_specs=pl.BlockSpec((1, S5_ROWS, S5_TN), lambda m, j: (m, 0, j)),
        out_shape=jax.ShapeDtypeStruct((3, S5_ROWS, S5_W), F32),
        compiler_params=_params("parallel", "parallel"),
        name="s5_chunk_in",
    )(u8, w3)


def _s5_scan_kernel(f_ref, b_ref, lam_ref, hf_ref, hb_ref, sf, sb, bf, bb):
    j = pl.program_id(0)

    @pl.when(j == 0)
    def _():
        sf[...] = jnp.zeros_like(sf)
        sb[...] = jnp.zeros_like(sb)

    half = S5_W // 2
    lam = lam_ref[...]
    lfr, lfi = lam[0:1, :half], lam[0:1, half:]
    lbr, lbi = lam[1:2, :half], lam[1:2, half:]

    def cmul(lr, li, h):
        hr, hi = h[:, :half], h[:, half:]
        return jnp.concatenate([lr * hr - li * hi, lr * hi + li * hr], axis=1)

    def body(r, carry):
        hf, hb = carry
        bf[pl.ds(r, 1), :] = hf
        hf = cmul(lfr, lfi, hf) + f_ref[0, pl.ds(r, 1), :]
        rb = S5_SB - 1 - r
        bb[pl.ds(rb, 1), :] = hb
        hb = cmul(lbr, lbi, hb) + b_ref[0, pl.ds(rb, 1), :]
        return hf, hb

    hf, hb = lax.fori_loop(0, S5_SB, body, (sf[...], sb[...]))
    sf[...] = hf
    sb[...] = hb
    hf_ref[...] = bf[...].astype(BF16)
    hb_ref[...] = bb[...].astype(BF16)


def _s5_scan(fb, lam8):
    fidx = lambda j: jnp.where(j == 0, S5_NSB - 1, j - 1)
    bidx = lambda j: S5_NSB - 1 - j
    o = jax.ShapeDtypeStruct((S5_ROWS, S5_W), BF16)
    return pl.pallas_call(
        _s5_scan_kernel,
        grid=(S5_NSB,),
        in_specs=[pl.BlockSpec((1, S5_SB, S5_W), lambda j: (1, fidx(j), 0)),
                  pl.BlockSpec((1, S5_SB, S5_W), lambda j: (2, bidx(j), 0)),
                  _const_spec((2, S5_W))],
        out_specs=[pl.BlockSpec((S5_SB, S5_W), lambda j: (fidx(j), 0)),
                   pl.BlockSpec((S5_SB, S5_W), lambda j: (bidx(j), 0))],
        out_shape=[o, o],
        scratch_shapes=[pltpu.VMEM((1, S5_W), F32), pltpu.VMEM((1, S5_W), F32),
                        pltpu.VMEM((S5_SB, S5_W), F32), pltpu.VMEM((S5_SB, S5_W), F32)],
        compiler_params=_params("arbitrary"),
        name="s5_scan",
    )(fb, fb, lam8)


def _s5_out_kernel(y_ref, hf_ref, hb_ref, wf_ref, wb_ref, o_ref):
    o_ref[...] = (y_ref[0] + jnp.dot(hf_ref[...], wf_ref[...], preferred_element_type=F32)
                  + jnp.dot(hb_ref[...], wb_ref[...], preferred_element_type=F32))


def _s5_out(fb, hf, hb, wf, wb):
    wblk = pl.BlockSpec((S5_W, S5_TN), lambda j: (0, j))
    return pl.pallas_call(
        _s5_out_kernel,
        grid=(S5_W // S5_TN,),
        in_specs=[pl.BlockSpec((1, S5_ROWS, S5_TN), lambda j: (0, 0, j)),
                  _const_spec((S5_ROWS, S5_W)), _const_spec((S5_ROWS, S5_W)), wblk, wblk],
        out_specs=pl.BlockSpec((S5_ROWS, S5_TN), lambda j: (0, j)),
        out_shape=jax.ShapeDtypeStruct((S5_ROWS, S5_W), F32),
        compiler_params=_params("parallel"),
        name="s5_chunk_out",
    )(fb, hf, hb, wf, wb)


def _value_rows(vt):
    return jnp.concatenate([vt, jnp.ones((ATT_VE - HEAD, vt.shape[1]), BF16)], axis=0)


def _attend(k, ve, q_ref, m_sc, acc_sc):
    def scores(g):
        j, c0 = divmod(g * ATT_GQ, ATT_TQ)
        return jnp.dot(k, q_ref[HEAD * j:HEAD * (j + 1), c0:c0 + ATT_GQ], preferred_element_type=F32)

    s_next = scores(0)
    for g in range(ATT_NG):
        s = s_next
        if g + 1 < ATT_NG:
            s_next = scores(g + 1)
        m_prev = m_sc[g]
        m_new = jnp.maximum(m_prev, jnp.max(s, axis=0, keepdims=True))
        alpha = jnp.exp2(m_prev - m_new)
        p = jnp.exp2(s - m_new).astype(BF16)
        acc_sc[g] = alpha * acc_sc[g] + jnp.dot(ve, p, preferred_element_type=F32)
        m_sc[g] = m_new


def _flash_kernel(q_ref, k_ref, v_ref, kc_ref, vc_ref, o_ref, m_sc, acc_sc):
    ki = pl.program_id(2)

    @pl.when(ki == 0)
    def _():
        m_sc[...] = jnp.full_like(m_sc, -1e30)
        acc_sc[...] = jnp.zeros_like(acc_sc)

    _attend(k_ref[0], _value_rows(v_ref[...]), q_ref, m_sc, acc_sc)

    @pl.when(ki == pl.num_programs(2) - 1)
    def _():
        _attend(kc_ref[0], _value_rows(vc_ref[...]), q_ref, m_sc, acc_sc)
        for g in range(ATT_NG):
            j, c0 = divmod(g * ATT_GQ, ATT_TQ)
            acc = acc_sc[g]
            o_ref[HEAD * j:HEAD * (j + 1), c0:c0 + ATT_GQ] = (acc[:HEAD] / acc[HEAD:HEAD + 1]).astype(BF16)


def _flash(qt, k, vt):
    return pl.pallas_call(
        _flash_kernel,
        grid=(KVH, L // ATT_TQ, L // ATT_TK),
        in_specs=[pl.BlockSpec((2 * HEAD, ATT_TQ), lambda h, qi, ki: (h, qi)),
                  pl.BlockSpec((1, ATT_TK, HEAD), lambda h, qi, ki: (h, ki, 0)),
                  pl.BlockSpec((HEAD, ATT_TK), lambda h, qi, ki: (h, ki)),
                  pl.BlockSpec((1, LC, HEAD), lambda h, qi, ki: (h, L // LC, 0)),
                  pl.BlockSpec((HEAD, LC), lambda h, qi, ki: (h, L // LC))],
        out_specs=pl.BlockSpec((2 * HEAD, ATT_TQ), lambda h, qi, ki: (h, qi)),
        out_shape=jax.ShapeDtypeStruct((QH * HEAD, L), BF16),
        scratch_shapes=[pltpu.VMEM((ATT_NG, 1, ATT_GQ), F32), pltpu.VMEM((ATT_NG, ATT_VE, ATT_GQ), F32)],
        compiler_params=_params("parallel", "parallel", "arbitrary"),
        name="flash_attention",
    )(qt, k, vt, k, vt)


def _ctx_attn_kernel(q_ref, k_ref, v_ref, o_ref):
    ve = _value_rows(v_ref[...])
    for j in range(2):
        s = jnp.dot(k_ref[0], q_ref[HEAD * j:HEAD * (j + 1), :], preferred_element_type=F32)
        p = jnp.exp2(s - jnp.max(s, axis=0, keepdims=True)).astype(BF16)
        acc = jnp.dot(ve, p, preferred_element_type=F32)
        o_ref[HEAD * j:HEAD * (j + 1), :] = (acc[:HEAD] / acc[HEAD:HEAD + 1]).astype(BF16)


def _ctx_attn(qt, k, vt):
    return pl.pallas_call(
        _ctx_attn_kernel,
        grid=(KVH,),
        in_specs=[pl.BlockSpec((2 * HEAD, LC), lambda h: (h, L // LC)),
                  pl.BlockSpec((1, LC, HEAD), lambda h: (h, L // LC, 0)),
                  pl.BlockSpec((HEAD, LC), lambda h: (h, L // LC))],
        out_specs=pl.BlockSpec((2 * HEAD, LC), lambda h: (h, 0)),
        out_shape=jax.ShapeDtypeStruct((QH * HEAD, LC), BF16),
        compiler_params=_params("parallel"),
        name="ctx_attention",
    )(qt, k, vt)


def _gelu_tanh(x):
    return 0.5 * x * (1.0 + jnp.tanh(math.sqrt(2.0 / math.pi) * (x + 0.044715 * (x * x * x))))


def _post_kernel(x_ref, mod_ref, hyl_ref, hyc_ref, s5_ref, po_ref, atl_ref, atc_ref,
                 gw_ref, gb_ref, wo_ref, g1_ref, g2_ref, g3_ref, w1_ref, w2_ref, o_ref):
    is_ctx = pl.program_id(0) >= NLB
    m = mod_ref[0]
    g = _gelu_tanh(s5_ref[...])
    s5o = g * jax.nn.sigmoid(_bdot(g, gw_ref[...]) + gb_ref[...])
    hy = jnp.where(is_ctx, hyc_ref[...], hyl_ref[...])
    at = jnp.where(is_ctx, atc_ref[...], atl_ref[...]).astype(F32).T
    o = (_bdot(hy, wo_ref[0:256, :]) + _bdot(s5o, wo_ref[256:512, :]) + _bdot(po_ref[...], wo_ref[512:768, :])
         + _bdot(at, wo_ref[768:1024, :]))
    x = x_ref[...] + m[2:3] * _rms(o, g1_ref[...])
    h = _rms(x, g2_ref[...]) * (1.0 + m[4:5]) + m[3:4]
    f = jnp.dot(h.astype(BF16), w1_ref[...], preferred_element_type=F32)
    f = jnp.square(jnp.maximum(f, 0.0)).astype(BF16)
    f = jnp.dot(f, w2_ref[...], preferred_element_type=F32)
    o_ref[...] = x + m[5:6] * _rms(f, g3_ref[...])


def _post(x, mods, hy_l, hy_c, s5y, po, at_l, at_c, glu_w, glu_b, w_out, g1, g2, g3, w1, w2):
    row = lambda w: pl.BlockSpec((TR, w), lambda i: (i, 0))
    lat = lambda i: jnp.minimum(i, NLB - 1)
    return pl.pallas_call(
        _post_kernel,
        grid=(NRB,),
        in_specs=[row(D),
                  pl.BlockSpec((1, N_MOD, D), lambda i: (i // NLB, 0, 0)),
                  pl.BlockSpec((TR, 256), lambda i: (lat(i), 0)),
                  _const_spec((LC, 256)),
                  row(256), row(256),
                  pl.BlockSpec((QH * HEAD, TR), lambda i: (0, lat(i))),
                  _const_spec((QH * HEAD, LC)),
                  _const_spec((256, 256)), _const_spec((1, 256)), _const_spec((D, D)),
                  _const_spec((1, D)), _const_spec((1, D)), _const_spec((1, D)),
                  _const_spec((D, D_FF)), _const_spec((D_FF, D))],
        out_specs=row(D),
        out_shape=jax.ShapeDtypeStruct((LT, D), F32),
        compiler_params=_params("parallel"),
        name="post_mix_mlp",
    )(x, mods, hy_l, hy_c, s5y, po, at_l, at_c, glu_w, glu_b, w_out, g1, g2, g3, w1, w2)


def _dft_tables():
    n = jnp.arange(128, dtype=jnp.int32)
    m = ((2 * n[:, None] + 1) * n[None, :]) % (2 * FFT_N1)
    ang = m.astype(F32) * (2.0 * math.pi / (2 * FFT_N1))
    c1, s1 = jnp.cos(ang), jnp.sin(ang)
    m1 = jnp.concatenate([c1, -s1], axis=0)
    m1inv = jnp.concatenate([c1.T, -s1.T], axis=1) * (2.0 / FFT_N)
    kk = n[:, None, None] + FFT_N1 * n[None, :, None]
    mm = ((2 * kk + 1) * n[None, None, :]) % (2 * FFT_N)
    phi = mm.astype(F32) * (2.0 * math.pi / (2 * FFT_N))
    cm, sm = jnp.cos(phi), jnp.sin(phi)
    w = jnp.concatenate([jnp.concatenate([cm, sm], axis=2), jnp.concatenate([-sm, cm], axis=2)], axis=1)
    wt = jnp.swapaxes(w, 1, 2)
    nc = jnp.arange(LC, dtype=jnp.int32)
    mc = ((2 * nc[:, None] + 1) * nc[None, :]) % (4 * LC)
    angc = mc.astype(F32) * (2.0 * math.pi / (4 * LC))
    cc, sc = jnp.cos(angc), jnp.sin(angc)
    fc = jnp.concatenate([cc, -sc], axis=0)
    fi = jnp.concatenate([cc.T, -sc.T], axis=1) * (2.0 / (2 * LC))
    return m1.astype(BF16), m1inv.astype(BF16), w.astype(BF16), wt.astype(BF16), fc, fi


def _rope_tables():
    t = jnp.arange(L, dtype=jnp.int32)
    inv = ROPE_THETA ** (-jnp.arange(0, HEAD // 2, 2, dtype=F32) / (HEAD // 2))
    ang = jnp.concatenate([(t // GRID_W).astype(F32)[:, None] * inv[None, :],
                           (t % GRID_W).astype(F32)[:, None] * inv[None, :]], axis=-1)
    ang = jnp.concatenate([ang, jnp.zeros((LC, HEAD // 2), F32)], axis=0)
    c, s = jnp.cos(ang), jnp.sin(ang)
    return jnp.concatenate([c, c, c, c], axis=1), jnp.concatenate([-s, s, -s, s], axis=1)


def _s5_tables(a_re, a_im, log_dt, b_re, b_im, c_re, c_im, d):
    dt = jnp.exp(log_dt)[..., None]
    tau = jnp.arange(S5_T + 1, dtype=F32)[:, None, None, None]
    mag = jnp.exp(a_re * dt * tau)
    pr, pi = mag * jnp.cos(a_im * dt * tau), mag * jnp.sin(a_im * dt * tau)
    lam_re, lam_im = pr[1], pi[1]
    den = a_re * a_re + a_im * a_im
    nr, ni = lam_re - 1.0, lam_im
    cr = (nr * a_re + ni * a_im) / den
    ci = (ni * a_re - nr * a_im) / den
    bb_re = cr[..., None] * b_re - ci[..., None] * b_im
    bb_im = cr[..., None] * b_im + ci[..., None] * b_re
    cl_re = c_re[None] * pr[:, :, :, None, :] - c_im[None] * pi[:, :, :, None, :]
    cl_im = c_re[None] * pi[:, :, :, None, :] + c_im[None] * pr[:, :, :, None, :]
    kk = jnp.einsum('tdghp,dgpk->tdghk', cl_re, bb_re) - jnp.einsum('tdghp,dgpk->tdghk', cl_im, bb_im)
    eye_g = jnp.eye(S5_G, dtype=F32)
    k0 = kk[0, 0] + kk[0, 1] + d.reshape(S5_G, S5_H)[:, :, None] * jnp.eye(S5_H, dtype=F32)[None]
    kfull = jnp.concatenate([kk[1:S5_T, 1][::-1], k0[None], kk[1:S5_T, 0]], axis=0)
    st = jnp.arange(S5_T)
    kt = kfull[st[None, :] - st[:, None] + S5_T - 1]
    m_intra = jnp.einsum('stghk,gj->sgktjh', kt, eye_g).reshape(S5_W, S5_W)
    def state_in(pw_re, pw_im, d_):
        re = pw_re[:, :, :, None] * bb_re[d_][None] - pw_im[:, :, :, None] * bb_im[d_][None]
        im = pw_re[:, :, :, None] * bb_im[d_][None] + pw_im[:, :, :, None] * bb_re[d_][None]
        both = jnp.stack([re, im], axis=0)
        return jnp.einsum('rsgpk,gj->sgkrjp', both, eye_g).reshape(S5_W, S5_W)
    m_fst = state_in(pr[:S5_T, 0][::-1], pi[:S5_T, 0][::-1], 0)
    m_bst = state_in(pr[:S5_T, 1], pi[:S5_T, 1], 1)
    def state_out(cre, cim):
        both = jnp.stack([cre, -cim], axis=0)
        return jnp.einsum('rtghp,gj->rgptjh', both, eye_g).reshape(S5_W, S5_W)
    m_fout = state_out(cl_re[1:, 0], cl_im[1:, 0])
    m_bout = state_out(cl_re[1:, 1][::-1], cl_im[1:, 1][::-1])
    lam8 = jnp.concatenate([pr[S5_T].reshape(2, -1), pi[S5_T].reshape(2, -1)], axis=1)
    w3 = jnp.stack([m_intra, m_fst, m_bst], axis=0).astype(BF16)
    return w3, lam8, m_fout.astype(BF16), m_bout.astype(BF16)


def _filter_params(w1, b1, w2, b2, w3, decay):
    w1p = jnp.concatenate([w1, jnp.zeros((128 - w1.shape[0], HY_FFN), F32)], axis=0)
    w3r = w3.reshape(HY_FFN, HY_ORDER, 2, DG)
    return (w1p, b1.reshape(1, -1), w2, b2.reshape(1, -1),
            w3r[:, :, 0].reshape(HY_FFN, 512), w3r[:, :, 1].reshape(HY_FFN, 512),
            decay[:, 0].reshape(1, 512), decay[:, 1].reshape(1, 512))


def _block_diag(w):
    g, n, _ = w.shape
    return jnp.einsum('gcd,gj->gcjd', w, jnp.eye(g, dtype=w.dtype)).reshape(g * n, g * n)


def kernel(x, c, ctx, c_ctx, mod_w, mod_b, norm_pre_mix, norm_post_mix, norm_pre_mlp, norm_post_mlp,
           w_in, w_out, hy_conv_w, hy_conv_b, hy_ffn_w1, hy_ffn_b1, hy_ffn_w2, hy_ffn_b2, hy_ffn_w3,
           hy_decay, hy_bias, s5_a_re, s5_a_im, s5_log_dt, s5_b_re, s5_b_im, s5_c_re, s5_c_im, s5_d,
           s5_glu_w, s5_glu_b, pool_w, pool_scale, att_q_norm, att_k_norm, mlp_w1, mlp_w2):
    xs = jnp.concatenate([x[0], ctx[0]], axis=0)
    mods = _modulation(c, c_ctx, mod_w, mod_b)
    m1, m1inv, wk, wkt, fc, fi = _dft_tables()
    cs, sn = _rope_tables()
    perm = jnp.concatenate([jnp.arange(0, HEAD, 2), jnp.arange(1, HEAD, 2)])
    qcols = 1280 + (jnp.arange(QH)[:, None] * HEAD + perm[None, :]).reshape(-1)
    kcols = 1536 + (jnp.arange(KVH)[:, None] * HEAD + perm[None, :]).reshape(-1)
    cols = jnp.concatenate([jnp.arange(1280), qcols, kcols, jnp.arange(1664, IN_COLS)])
    head_mean = _block_diag(jnp.full((QH, HEAD, HEAD), 1.0 / HEAD, F32)).astype(BF16)

    for i in range(DEPTH):
        w_in_i = w_in[i][:, cols].astype(BF16)
        qg = jnp.tile(att_q_norm[i][perm], QH).reshape(1, 256)
        kg = jnp.tile(att_k_norm[i][perm], KVH).reshape(1, 128)
        a, s, p, q, k, v = _premix(xs, mods[i], norm_pre_mix[i].reshape(1, D), w_in_i, cs, sn, qg, kg, head_mean)

        vv, x1, x2, po = _local(a, p, hy_conv_w[i], hy_conv_b[i].reshape(1, 768),
                                _block_diag(pool_w[i]), pool_scale[i].reshape(1, 256))

        fp = _filter_params(hy_ffn_w1[i], hy_ffn_b1[i], hy_ffn_w2[i], hy_ffn_b2[i], hy_ffn_w3[i], hy_decay[i])
        filt = _filter_gen(fp)
        fa = _fft_fwd1(filt.reshape(FFT_R, FFT_N2, 1024), m1)
        kf = _filt_spec(fa, wk)
        time_major = lambda t: t.reshape(LT // FFT_N2, FFT_N2, 256)
        z = time_major(vv)
        gates = (time_major(x1), time_major(x2))
        for o in range(HY_ORDER):
            fa_z = _fft_fwd1(z, m1)
            bm = _fft_mid(fa_z, wk, wkt, kf, o)
            z = _fft_inv1(bm, m1inv, gates[o], z, hy_bias[i][o].reshape(1, 256))
        hy_l = z.reshape(L, 256)
        hy_c = _ctx_hyena(a, hy_conv_w[i], hy_conv_b[i].reshape(1, 768), fp, hy_bias[i], fc, fi)

        w3, lam8, m_fout, m_bout = _s5_tables(s5_a_re[i], s5_a_im[i], s5_log_dt[i], s5_b_re[i], s5_b_im[i],
                                              s5_c_re[i], s5_c_im[i], s5_d[i])
        fb = _s5_in(s.reshape(S5_ROWS, S5_W), w3)
        hf, hb = _s5_scan(fb, lam8)
        s5y = _s5_out(fb, hf, hb, m_fout, m_bout).reshape(LT, 256)

        at_l = _flash(q, k, v)
        at_c = _ctx_attn(q, k, v)

        xs = _post(xs, mods[i], hy_l, hy_c, s5y, po, at_l, at_c,
                   s5_glu_w[i].astype(BF16), s5_glu_b[i].reshape(1, 256), w_out[i].astype(BF16),
                   norm_post_mix[i].reshape(1, D), norm_pre_mlp[i].reshape(1, D), norm_post_mlp[i].reshape(1, D),
                   mlp_w1[i].astype(BF16), mlp_w2[i].astype(BF16))
    return xs[:L][None]
```

```python
import functools
import math

import jax
import jax.numpy as jnp
from jax import lax
from jax.experimental import pallas as pl
from jax.experimental.pallas import tpu as pltpu

F32 = jnp.float32
BF16 = jnp.bfloat16

D = 1024
L = 16384
LC = 256
LT = L + LC
DEPTH = 4
GRID_W = 64
EPS = 1e-6
N_MOD = 6
DG = 256
HY_ORDER = 2
HY_FREQS = 16
HY_FFN = 64
HY_SIN_FREQ = 1.0
S5_H = 16
S5_G = 16
S5_P = 64
HEAD = 64
QH = 4
KVH = 2
ATT_SCALE = 1.0 / math.sqrt(HEAD)
ROPE_THETA = 10000.0
D_FF = 4 * D
IN_COLS = 1792

TR = 256
NLB = L // TR
NRB = LT // TR
HALO = 8

FFT_N = 2 * L
FFT_N1 = 256
FFT_N2 = 128
FFT_K1 = FFT_N1 // 2
FFT_R = L // FFT_N2
FFT_TJ = 8
FFT_TK = 8

S5_T = 8
S5_W = S5_T * DG
S5_ROWS = LT // S5_T
S5_SB = LC // S5_T
S5_NSB = S5_ROWS // S5_SB
S5_RB = 416

ATT_TQ = 1024
ATT_TK = 4096
ATT_KS = 4096
ATT_GQ = 512
ATT_NG = 2 * ATT_TQ // ATT_GQ
ATT_VE = HEAD + 16
LOG2E = math.log2(math.e)

VMEM_LIMIT = 56 * 1024 * 1024


def _params(*sem):
    return pltpu.CompilerParams(dimension_semantics=sem, vmem_limit_bytes=VMEM_LIMIT)


def _const_spec(shape):
    nd = len(shape)
    return pl.BlockSpec(shape, lambda *_: (0,) * nd, pipeline_mode=pl.Buffered(1))


def _bdot(a, b):
    return jnp.dot(a.astype(BF16), b.astype(BF16), preferred_element_type=F32)


def _split(a):
    hi = a.astype(BF16)
    lo = (a - hi.astype(F32)).astype(BF16)
    return hi, lo


def _dot3(a, b):
    ah, al = _split(a)
    bh, bl = _split(b)
    return (jnp.dot(ah, bh, preferred_element_type=F32)
            + jnp.dot(al, bh, preferred_element_type=F32)
            + jnp.dot(ah, bl, preferred_element_type=F32))


def _rms(x, g):
    return x * lax.rsqrt(jnp.mean(x * x, axis=-1, keepdims=True) + EPS) * g


def _mod_kernel(c_ref, w_ref, b_ref, o_ref):
    c = c_ref[...]
    s = c * jax.nn.sigmoid(c)
    o_ref[0] = _dot3(s, w_ref[0]) + b_ref[0]


def _modulation(c, c_ctx, mod_w, mod_b):
    cc = jnp.concatenate([c.reshape(1, D), c_ctx.reshape(1, D), jnp.zeros((6, D), F32)], axis=0)
    tn = 1536
    out = pl.pallas_call(
        _mod_kernel,
        grid=(DEPTH, N_MOD * D // tn),
        in_specs=[pl.BlockSpec((8, D), lambda i, j: (0, 0)),
                  pl.BlockSpec((1, D, tn), lambda i, j: (i, 0, j)),
                  pl.BlockSpec((1, 1, tn), lambda i, j: (i, 0, j))],
        out_specs=pl.BlockSpec((1, 8, tn), lambda i, j: (i, 0, j)),
        out_shape=jax.ShapeDtypeStruct((DEPTH, 8, N_MOD * D), F32),
        compiler_params=_params("parallel", "parallel"),
        name="modulation",
    )(cc, mod_w, mod_b.reshape(DEPTH, 1, N_MOD * D))
    return out[:, :2].reshape(DEPTH, 2, N_MOD, D)


def _qknorm_rope(t, gain, cs, sn, hm):
    hi, lo = _split(t * t)
    ms = jnp.dot(hi, hm, preferred_element_type=F32) + jnp.dot(lo, hm, preferred_element_type=F32)
    tn = t * lax.rsqrt(ms + EPS) * gain
    w = t.shape[1]
    lane = lax.broadcasted_iota(jnp.int32, tn.shape, 1)
    swapped = jnp.where((lane % HEAD) < HEAD // 2, pltpu.roll(tn, w - HEAD // 2, 1), pltpu.roll(tn, HEAD // 2, 1))
    return tn * cs + swapped * sn


def _premix_kernel(x_ref, mod_ref, g_ref, w_ref, cs_ref, sn_ref, qg_ref, kg_ref, hm_ref,
                   a_ref, s_ref, p_ref, q_ref, k_ref, v_ref):
    m = mod_ref[0]
    h = _rms(x_ref[...], g_ref[...]) * (1.0 + m[1:2]) + m[0:1]
    u = jnp.dot(h.astype(BF16), w_ref[...], preferred_element_type=F32)
    a_ref[...] = u[:, 0:768]
    s_ref[...] = u[:, 768:1024]
    p_ref[...] = u[:, 1024:1280]
    cs = cs_ref[...]
    sn = sn_ref[...]
    hm = hm_ref[...]
    q = _qknorm_rope(u[:, 1280:1536], qg_ref[...], jnp.concatenate([cs, cs], axis=1),
                     jnp.concatenate([sn, sn], axis=1), hm) * (ATT_SCALE * LOG2E)
    k = _qknorm_rope(u[:, 1536:1664], kg_ref[...], cs, sn, hm[:128, :128])
    q_ref[...] = q.T.astype(BF16)
    v_ref[...] = u[:, 1664:1792].T.astype(BF16)
    for hh in range(KVH):
        k_ref[hh] = k[:, HEAD * hh:HEAD * (hh + 1)].astype(BF16)


def _premix(x, mods, g, w_in, cs, sn, qg, kg, hm):
    row = lambda w: pl.BlockSpec((TR, w), lambda i: (i, 0))
    return pl.pallas_call(
        _premix_kernel,
        grid=(NRB,),
        in_specs=[row(D),
                  pl.BlockSpec((1, N_MOD, D), lambda i: (i // NLB, 0, 0)),
                  _const_spec((1, D)), _const_spec((D, IN_COLS)),
                  row(128), row(128),
                  _const_spec((1, 256)), _const_spec((1, 128)), _const_spec((256, 256))],
        out_specs=[row(768), row(256), row(256),
                   pl.BlockSpec((QH * HEAD, TR), lambda i: (0, i)),
                   pl.BlockSpec((KVH, TR, HEAD), lambda i: (0, i, 0)),
                   pl.BlockSpec((KVH * HEAD, TR), lambda i: (0, i))],
        out_shape=[jax.ShapeDtypeStruct((LT, 768), F32),
                   jax.ShapeDtypeStruct((LT, 256), F32),
                   jax.ShapeDtypeStruct((LT, 256), F32),
                   jax.ShapeDtypeStruct((QH * HEAD, LT), BF16),
                   jax.ShapeDtypeStruct((KVH, LT, HEAD), BF16),
                   jax.ShapeDtypeStruct((KVH * HEAD, LT), BF16)],
        compiler_params=_params("parallel"),
        name="premix",
    )(x, mods, g, w_in, cs, sn, qg, kg, hm)


def _local_kernel(a_ref, ap_ref, an_ref, p_ref, pp_ref, pn_ref, cw_ref, cb_ref, pw_ref, ps_ref,
                  v_ref, x1_ref, x2_ref, po_ref):
    i = pl.program_id(0)
    is_ctx = i >= NLB
    s0 = jnp.where(is_ctx, L, 0)
    s1 = jnp.where(is_ctx, LT, L)
    n = TR + 2 * HALO
    idx = i * TR - HALO + lax.broadcasted_iota(jnp.int32, (n, 1), 0)
    valid = (idx >= s0) & (idx < s1)

    def shifted(e, d):
        return pltpu.roll(e, (-d) % n, 0)[HALO:HALO + TR] if d else e[HALO:HALO + TR]

    ea = jnp.where(valid, jnp.concatenate([ap_ref[...], a_ref[...], an_ref[...]], axis=0), 0.0)
    w = cw_ref[...]
    y = cb_ref[...] + w[0:1] * shifted(ea, -1) + w[1:2] * shifted(ea, 0) + w[2:3] * shifted(ea, 1)
    v_ref[...] = y[:, 0:256]
    x1_ref[...] = y[:, 256:512]
    x2_ref[...] = y[:, 512:768]

    e = jnp.where(valid, jnp.concatenate([pp_ref[...], p_ref[...], pn_ref[...]], axis=0), 0.0)
    w2 = e + pltpu.roll(e, 1, 0)
    w4 = pltpu.roll(w2, 1, 0) + pltpu.roll(w2, n - 1, 0)
    w8 = pltpu.roll(w4, 2, 0) + pltpu.roll(w4, n - 2, 0)
    w16 = pltpu.roll(w8, 4, 0) + pltpu.roll(w8, n - 4, 0)
    lane = lax.broadcasted_iota(jnp.int32, (TR, 256), 1)
    cut = lambda t: t[HALO:HALO + TR]
    sums = jnp.where(lane < 64, cut(w2), jnp.where(lane < 128, cut(w4), jnp.where(lane < 192, cut(w8), cut(w16))))
    half = jnp.where(lane < 64, 1, jnp.where(lane < 128, 2, jnp.where(lane < 192, 4, 8)))
    tl = i * TR + lax.broadcasted_iota(jnp.int32, (TR, 256), 0) - s0
    cnt = jnp.minimum(tl + half, s1 - s0) - jnp.maximum(tl - half, 0)
    pooled = sums / cnt.astype(F32) - p_ref[...]
    po_ref[...] = _bdot(pooled, pw_ref[...]) * ps_ref[...]


def _local(a, p, conv_w, conv_b, pool_bd, pool_scale):
    r8 = TR // HALO
    main = lambda w: pl.BlockSpec((TR, w), lambda i: (i, 0))
    prev = lambda w: pl.BlockSpec((HALO, w), lambda i: (jnp.maximum(i * r8 - 1, 0), 0))
    nxt = lambda w: pl.BlockSpec((HALO, w), lambda i: (jnp.minimum((i + 1) * r8, LT // HALO - 1), 0))
    o = jax.ShapeDtypeStruct((LT, 256), F32)
    return pl.pallas_call(
        _local_kernel,
        grid=(NRB,),
        in_specs=[main(768), prev(768), nxt(768), main(256), prev(256), nxt(256),
                  _const_spec((3, 768)), _const_spec((1, 768)), _const_spec((256, 256)), _const_spec((1, 256))],
        out_specs=[main(256)] * 4,
        out_shape=[o, o, o, o],
        compiler_params=_params("parallel"),
        name="local_ops",
    )(a, a, a, p, p, p, conv_w, conv_b, pool_bd, pool_scale)


def _filter_hidden(n, inv_len, w1, b1, w2, b2):
    t = n * inv_len
    lane = lax.broadcasted_iota(jnp.int32, (n.shape[0], 128), 1)
    fr = jnp.where(lane <= HY_FREQS, lane, lane - HY_FREQS).astype(F32)
    ang = (2.0 * math.pi * t) * fr
    feats = jnp.where(lane == 0, t,
                      jnp.where(lane <= HY_FREQS, jnp.cos(ang),
                                jnp.where(lane <= 2 * HY_FREQS, jnp.sin(ang), 0.0)))
    h = jnp.sin(HY_SIN_FREQ * (_dot3(feats, w1) + b1))
    h = jnp.sin(HY_SIN_FREQ * (_dot3(h, w2) + b2))
    return h, t


def _filter_block(n0, rows, inv_len, w1, b1, w2, b2, w3f, w3b, dec_f, dec_b):
    n_ext = (n0 - HALO + lax.broadcasted_iota(jnp.int32, (rows + HALO, 1), 0)).astype(F32)
    h_ext, t_ext = _filter_hidden(n_ext, inv_len, w1, b1, w2, b2)
    n, t = n_ext[HALO:], t_ext[HALO:]
    hf = _dot3(h_ext[HALO:], w3f) * jnp.exp(-t * jnp.abs(dec_f))
    tp = t - inv_len
    hb = _dot3(pltpu.roll(h_ext, 1, 0)[HALO:], w3b) * jnp.exp(-tp * jnp.abs(dec_b))
    hb = jnp.where(n >= 1.0, hb, 0.0)
    return hf, hb


def _filter_kernel(w1_ref, b1_ref, w2_ref, b2_ref, w3f_ref, w3b_ref, df_ref, db_ref, o_ref, *, rows):
    hf, hb = _filter_block(pl.program_id(0) * rows, rows, 1.0 / L, w1_ref[...], b1_ref[...], w2_ref[...],
                           b2_ref[...], w3f_ref[...], w3b_ref[...], df_ref[...], db_ref[...])
    o_ref[:, 0:256] = hf[:, 0:256]
    o_ref[:, 256:512] = hb[:, 0:256]
    o_ref[:, 512:768] = hf[:, 256:512]
    o_ref[:, 768:1024] = hb[:, 256:512]


def _filter_gen(fp):
    rows = 512
    return pl.pallas_call(
        functools.partial(_filter_kernel, rows=rows),
        grid=(L // rows,),
        in_specs=[_const_spec((128, HY_FFN)), _const_spec((1, HY_FFN)), _const_spec((HY_FFN, HY_FFN)),
                  _const_spec((1, HY_FFN)), _const_spec((HY_FFN, 512)), _const_spec((HY_FFN, 512)),
                  _const_spec((1, 512)), _const_spec((1, 512))],
        out_specs=pl.BlockSpec((rows, 1024), lambda i: (i, 0)),
        out_shape=jax.ShapeDtypeStruct((L, 1024), F32),
        compiler_params=_params("parallel"),
        name="hyena_filter",
    )(*fp)


def _fft_fwd1_kernel(x_ref, m_ref, o_ref):
    for jj in range(FFT_TJ):
        res = jnp.dot(m_ref[...], x_ref[:, jj, :].astype(BF16), preferred_element_type=F32)
        o_ref[0, :, jj, :] = res[:FFT_K1]
        o_ref[1, :, jj, :] = res[FFT_K1:]


def _fft_fwd1(x3, m1):
    c = x3.shape[2]
    return pl.pallas_call(
        _fft_fwd1_kernel,
        grid=(FFT_N2 // FFT_TJ, c // 256),
        in_specs=[pl.BlockSpec((FFT_R, FFT_TJ, 256), lambda j, cc: (0, j, cc)), _const_spec((FFT_N1, FFT_R))],
        out_specs=pl.BlockSpec((2, FFT_K1, FFT_TJ, 256), lambda j, cc: (0, 0, j, cc)),
        out_shape=jax.ShapeDtypeStruct((2, FFT_K1, FFT_N2, c), F32),
        compiler_params=_params("parallel", "parallel"),
        name="fft_fwd1",
    )(x3, m1)


def _filt_spec_kernel(a_ref, w_ref, o_ref):
    for j in range(FFT_TK):
        a = jnp.concatenate([a_ref[0, j], a_ref[1, j]], axis=0).astype(BF16)
        x = jnp.dot(w_ref[j], a, preferred_element_type=F32)
        o_ref[0, 0, j] = x[:FFT_N2, 0:256] + x[:FFT_N2, 256:512]
        o_ref[0, 1, j] = x[FFT_N2:, 0:256] - x[FFT_N2:, 256:512]


def _filt_spec(fa, w):
    return pl.pallas_call(
        _filt_spec_kernel,
        grid=(HY_ORDER, FFT_K1 // FFT_TK),
        in_specs=[pl.BlockSpec((2, FFT_TK, FFT_N2, 512), lambda o, i: (0, i, 0, o)),
                  pl.BlockSpec((FFT_TK, 256, 256), lambda o, i: (i, 0, 0))],
        out_specs=pl.BlockSpec((1, 2, FFT_TK, FFT_N2, 256), lambda o, i: (o, 0, i, 0, 0)),
        out_shape=jax.ShapeDtypeStruct((HY_ORDER, 2, FFT_K1, FFT_N2, 256), F32),
        compiler_params=_params("parallel", "parallel"),
        name="hyena_filter_spectrum",
    )(fa, w)


def _fft_mid_kernel(a_ref, w_ref, wt_ref, k_ref, o_ref):
    for j in range(FFT_TK):
        a = jnp.concatenate([a_ref[0, j], a_ref[1, j]], axis=0).astype(BF16)
        x = jnp.dot(w_ref[j], a, preferred_element_type=F32)
        xr, xi = x[:FFT_N2], x[FFT_N2:]
        kr, ki = k_ref[0, 0, j], k_ref[0, 1, j]
        y = jnp.concatenate([xr * kr - xi * ki, xr * ki + xi * kr], axis=0).astype(BF16)
        b = jnp.dot(wt_ref[j], y, preferred_element_type=F32)
        o_ref[0, j] = b[:FFT_N2]
        o_ref[1, j] = b[FFT_N2:]


def _fft_mid(a, w, wt, kf, order):
    blk = pl.BlockSpec((2, FFT_TK, FFT_N2, 256), lambda i: (0, i, 0, 0))
    wblk = pl.BlockSpec((FFT_TK, 256, 256), lambda i: (i, 0, 0))
    return pl.pallas_call(
        _fft_mid_kernel,
        grid=(FFT_K1 // FFT_TK,),
        in_specs=[blk, wblk, wblk,
                  pl.BlockSpec((1, 2, FFT_TK, FFT_N2, 256), lambda i: (order, 0, i, 0, 0))],
        out_specs=blk,
        out_shape=jax.ShapeDtypeStruct((2, FFT_K1, FFT_N2, 256), F32),
        compiler_params=_params("parallel"),
        name="fft_mid",
    )(a, w, wt, kf)


def _fft_inv1_kernel(b_ref, m_ref, x_ref, z_ref, fb_ref, o_ref):
    for jj in range(FFT_TJ):
        b = jnp.concatenate([b_ref[0, :, jj, :], b_ref[1, :, jj, :]], axis=0).astype(BF16)
        y = jnp.dot(m_ref[...], b, preferred_element_type=F32)
        o_ref[:, jj, :] = x_ref[:, jj, :] * (y + fb_ref[...] * z_ref[:, jj, :])


def _fft_inv1(b4, minv, gate3, z3, fb):
    sig = pl.BlockSpec((FFT_R, FFT_TJ, 256), lambda j: (0, j, 0))
    return pl.pallas_call(
        _fft_inv1_kernel,
        grid=(FFT_N2 // FFT_TJ,),
        in_specs=[pl.BlockSpec((2, FFT_K1, FFT_TJ, 256), lambda j: (0, 0, j, 0)), _const_spec((FFT_R, FFT_N1)),
                  sig, sig, _const_spec((1, 256))],
        out_specs=sig,
        out_shape=jax.ShapeDtypeStruct((FFT_R, FFT_N2, 256), F32),
        compiler_params=_params("parallel"),
        name="fft_inv1_gate",
    )(b4, minv, gate3, z3, fb)


def _ctx_hyena_kernel(a_ref, cw_ref, cb_ref, w1_ref, b1_ref, w2_ref, b2_ref, w3f_ref, w3b_ref, df_ref, db_ref,
                      fb_ref, fc_ref, fi_ref, o_ref):
    zero = jnp.zeros((HALO, 768), F32)
    n = LC + 2 * HALO
    ea = jnp.concatenate([zero, a_ref[...], zero], axis=0)
    w = cw_ref[...]
    y = (cb_ref[...] + w[0:1] * pltpu.roll(ea, 1, 0)[HALO:HALO + LC] + w[1:2] * ea[HALO:HALO + LC]
         + w[2:3] * pltpu.roll(ea, n - 1, 0)[HALO:HALO + LC])
    hf, hb = _filter_block(0, LC, 1.0 / LC, w1_ref[...], b1_ref[...], w2_ref[...], b2_ref[...],
                           w3f_ref[...], w3b_ref[...], df_ref[...], db_ref[...])
    fc = fc_ref[...]
    fi = fi_ref[...]
    sf = _dot3(fc, hf)
    sb = _dot3(fc, hb)
    kr = sf[:LC] + sb[:LC]
    ki = sf[LC:] - sb[LC:]
    z = y[:, 0:256]
    fb = fb_ref[...]
    for o in range(HY_ORDER):
        s = _dot3(fc, z)
        sr, si = s[:LC], s[LC:]
        c0 = 256 * o
        yr = sr * kr[:, c0:c0 + 256] - si * ki[:, c0:c0 + 256]
        yi = sr * ki[:, c0:c0 + 256] + si * kr[:, c0:c0 + 256]
        conv = _dot3(fi, jnp.concatenate([yr, yi], axis=0))
        z = y[:, 256 * (o + 1):256 * (o + 2)] * (conv + fb[o:o + 1] * z)
    o_ref[...] = z


def _ctx_hyena(a, conv_w, conv_b, fp, fbias, fc, fi):
    return pl.pallas_call(
        _ctx_hyena_kernel,
        grid=(1,),
        in_specs=[pl.BlockSpec((LC, 768), lambda i: (L // LC, 0)),
                  _const_spec((3, 768)), _const_spec((1, 768)),
                  _const_spec((128, HY_FFN)), _const_spec((1, HY_FFN)), _const_spec((HY_FFN, HY_FFN)),
                  _const_spec((1, HY_FFN)), _const_spec((HY_FFN, 512)), _const_spec((HY_FFN, 512)),
                  _const_spec((1, 512)), _const_spec((1, 512)),
                  _const_spec((HY_ORDER, 256)), _const_spec((2 * LC, LC)), _const_spec((LC, 2 * LC))],
        out_specs=pl.BlockSpec((LC, 256), lambda i: (0, 0)),
        out_shape=jax.ShapeDtypeStruct((LC, 256), F32),
        compiler_params=_params("arbitrary"),
        name="hyena_ctx",
    )(a, conv_w, conv_b, *fp, fbias, fc, fi)


def _s5_in_kernel(u_ref, w_ref, o_ref):
    acc = jnp.dot(u_ref[:, 0, :].astype(BF16), w_ref[0, 0:DG, :], preferred_element_type=F32)
    for j in range(1, S5_T):
        acc = acc + jnp.dot(u_ref[:, j, :].astype(BF16), w_ref[0, DG * j:DG * (j + 1), :],
                            preferred_element_type=F32)
    o_ref[0] = acc


def _s5_in(u3, w3):
    return pl.pallas_call(
        _s5_in_kernel,
        grid=(3, S5_ROWS // S5_RB),
        in_specs=[pl.BlockSpec((S5_RB, S5_T, DG), lambda m, r: (r, 0, 0)),
                  pl.BlockSpec((1, S5_W, S5_W), lambda m, r: (m, 0, 0))],
        out_specs=pl.BlockSpec((1, S5_RB, S5_W), lambda m, r: (m, r, 0)),
        out_shape=jax.ShapeDtypeStruct((3, S5_ROWS, S5_W), F32),
        compiler_params=_params("parallel", "parallel"),
        name="s5_chunk_in",
    )(u3, w3)


def _s5_scan_kernel(f_ref, b_ref, lam_ref, hf_ref, hb_ref, sf, sb, bf, bb):
    j = pl.program_id(0)

    @pl.when(j == 0)
    def _():
        sf[...] = jnp.zeros_like(sf)
        sb[...] = jnp.zeros_like(sb)

    half = S5_W // 2
    lam = lam_ref[...]
    lfr, lfi = lam[0:1, :half], lam[0:1, half:]
    lbr, lbi = lam[1:2, :half], lam[1:2, half:]

    def cmul(lr, li, h):
        hr, hi = h[:, :half], h[:, half:]
        return jnp.concatenate([lr * hr - li * hi, lr * hi + li * hr], axis=1)

    def body(r, carry):
        hf, hb = carry
        bf[pl.ds(r, 1), :] = hf
        hf = cmul(lfr, lfi, hf) + f_ref[0, pl.ds(r, 1), :]
        rb = S5_SB - 1 - r
        bb[pl.ds(rb, 1), :] = hb
        hb = cmul(lbr, lbi, hb) + b_ref[0, pl.ds(rb, 1), :]
        return hf, hb

    hf, hb = lax.fori_loop(0, S5_SB, body, (sf[...], sb[...]))
    sf[...] = hf
    sb[...] = hb
    hf_ref[...] = bf[...].astype(BF16)
    hb_ref[...] = bb[...].astype(BF16)


def _s5_scan(fb, lam8):
    fidx = lambda j: jnp.where(j == 0, S5_NSB - 1, j - 1)
    bidx = lambda j: S5_NSB - 1 - j
    o = jax.ShapeDtypeStruct((S5_ROWS, S5_W), BF16)
    return pl.pallas_call(
        _s5_scan_kernel,
        grid=(S5_NSB,),
        in_specs=[pl.BlockSpec((1, S5_SB, S5_W), lambda j: (1, fidx(j), 0)),
                  pl.BlockSpec((1, S5_SB, S5_W), lambda j: (2, bidx(j), 0)),
                  _const_spec((2, S5_W))],
        out_specs=[pl.BlockSpec((S5_SB, S5_W), lambda j: (fidx(j), 0)),
                   pl.BlockSpec((S5_SB, S5_W), lambda j: (bidx(j), 0))],
        out_shape=[o, o],
        scratch_shapes=[pltpu.VMEM((1, S5_W), F32), pltpu.VMEM((1, S5_W), F32),
                        pltpu.VMEM((S5_SB, S5_W), F32), pltpu.VMEM((S5_SB, S5_W), F32)],
        compiler_params=_params("arbitrary"),
        name="s5_scan",
    )(fb, fb, lam8)


def _s5_out_kernel(y_ref, hf_ref, hb_ref, wf_ref, wb_ref, o_ref):
    y = (y_ref[0] + jnp.dot(hf_ref[...], wf_ref[...], preferred_element_type=F32)
         + jnp.dot(hb_ref[...], wb_ref[...], preferred_element_type=F32))
    for j in range(S5_T):
        o_ref[:, j, :] = y[:, DG * j:DG * (j + 1)]


def _s5_out(fb, hf, hb, wf, wb):
    rows = pl.BlockSpec((S5_RB, S5_W), lambda r: (r, 0))
    return pl.pallas_call(
        _s5_out_kernel,
        grid=(S5_ROWS // S5_RB,),
        in_specs=[pl.BlockSpec((1, S5_RB, S5_W), lambda r: (0, r, 0)), rows, rows,
                  _const_spec((S5_W, S5_W)), _const_spec((S5_W, S5_W))],
        out_specs=pl.BlockSpec((S5_RB, S5_T, DG), lambda r: (r, 0, 0)),
        out_shape=jax.ShapeDtypeStruct((S5_ROWS, S5_T, DG), F32),
        compiler_params=_params("parallel"),
        name="s5_chunk_out",
    )(fb, hf, hb, wf, wb)


def _value_rows(vt):
    return jnp.concatenate([vt, jnp.ones((ATT_VE - HEAD, vt.shape[1]), BF16)], axis=0)


def _attend(k, ve, q_ref, m_sc, acc_sc):
    ks = min(ATT_KS, k.shape[0])
    units = [(c, g) for c in range(k.shape[0] // ks) for g in range(ATT_NG)]

    def scores(unit):
        c, g = unit
        j, c0 = divmod(g * ATT_GQ, ATT_TQ)
        return jnp.dot(k[c * ks:(c + 1) * ks], q_ref[HEAD * j:HEAD * (j + 1), c0:c0 + ATT_GQ],
                       preferred_element_type=F32)

    s_next = scores(units[0])
    for u, (c, g) in enumerate(units):
        s = s_next
        if u + 1 < len(units):
            s_next = scores(units[u + 1])
        m_prev = m_sc[g]
        m_new = jnp.maximum(m_prev, jnp.max(s, axis=0, keepdims=True))
        alpha = jnp.exp2(m_prev - m_new)
        p = jnp.exp2(s - m_new).astype(BF16)
        acc_sc[g] = alpha * acc_sc[g] + jnp.dot(ve[:, c * ks:(c + 1) * ks], p, preferred_element_type=F32)
        m_sc[g] = m_new


def _flash_kernel(q_ref, k_ref, v_ref, kc_ref, vc_ref, o_ref, m_sc, acc_sc):
    ki = pl.program_id(2)

    @pl.when(ki == 0)
    def _():
        m_sc[...] = jnp.full_like(m_sc, -1e30)
        acc_sc[...] = jnp.zeros_like(acc_sc)

    _attend(k_ref[0], _value_rows(v_ref[...]), q_ref, m_sc, acc_sc)

    @pl.when(ki == pl.num_programs(2) - 1)
    def _():
        _attend(kc_ref[0], _value_rows(vc_ref[...]), q_ref, m_sc, acc_sc)
        for g in range(ATT_NG):
            j, c0 = divmod(g * ATT_GQ, ATT_TQ)
            acc = acc_sc[g]
            o_ref[HEAD * j:HEAD * (j + 1), c0:c0 + ATT_GQ] = (acc[:HEAD] / acc[HEAD:HEAD + 1]).astype(BF16)


def _flash(qt, k, vt):
    return pl.pallas_call(
        _flash_kernel,
        grid=(KVH, L // ATT_TQ, L // ATT_TK),
        in_specs=[pl.BlockSpec((2 * HEAD, ATT_TQ), lambda h, qi, ki: (h, qi)),
                  pl.BlockSpec((1, ATT_TK, HEAD), lambda h, qi, ki: (h, ki, 0)),
                  pl.BlockSpec((HEAD, ATT_TK), lambda h, qi, ki: (h, ki)),
                  pl.BlockSpec((1, LC, HEAD), lambda h, qi, ki: (h, L // LC, 0)),
                  pl.BlockSpec((HEAD, LC), lambda h, qi, ki: (h, L // LC))],
        out_specs=pl.BlockSpec((2 * HEAD, ATT_TQ), lambda h, qi, ki: (h, qi)),
        out_shape=jax.ShapeDtypeStruct((QH * HEAD, L), BF16),
        scratch_shapes=[pltpu.VMEM((ATT_NG, 1, ATT_GQ), F32), pltpu.VMEM((ATT_NG, ATT_VE, ATT_GQ), F32)],
        compiler_params=_params("parallel", "parallel", "arbitrary"),
        name="flash_attention",
    )(qt, k, vt, k, vt)


def _ctx_attn_kernel(q_ref, k_ref, v_ref, o_ref):
    ve = _value_rows(v_ref[...])
    for j in range(2):
        s = jnp.dot(k_ref[0], q_ref[HEAD * j:HEAD * (j + 1), :], preferred_element_type=F32)
        p = jnp.exp2(s - jnp.max(s, axis=0, keepdims=True)).astype(BF16)
        acc = jnp.dot(ve, p, preferred_element_type=F32)
        o_ref[HEAD * j:HEAD * (j + 1), :] = (acc[:HEAD] / acc[HEAD:HEAD + 1]).astype(BF16)


def _ctx_attn(qt, k, vt):
    return pl.pallas_call(
        _ctx_attn_kernel,
        grid=(KVH,),
        in_specs=[pl.BlockSpec((2 * HEAD, LC), lambda h: (h, L // LC)),
                  pl.BlockSpec((1, LC, HEAD), lambda h: (h, L // LC, 0)),
                  pl.BlockSpec((HEAD, LC), lambda h: (h, L // LC))],
        out_specs=pl.BlockSpec((2 * HEAD, LC), lambda h: (h, 0)),
        out_shape=jax.ShapeDtypeStruct((QH * HEAD, LC), BF16),
        compiler_params=_params("parallel"),
        name="ctx_attention",
    )(qt, k, vt)


def _gelu_tanh(x):
    return 0.5 * x * (1.0 + jnp.tanh(math.sqrt(2.0 / math.pi) * (x + 0.044715 * (x * x * x))))


def _post_kernel(x_ref, mod_ref, hyl_ref, hyc_ref, s5_ref, po_ref, atl_ref, atc_ref,
                 gw_ref, gb_ref, wo_ref, g1_ref, g2_ref, g3_ref, w1_ref, w2_ref, o_ref):
    is_ctx = pl.program_id(0) >= NLB
    m = mod_ref[0]
    g = _gelu_tanh(s5_ref[...])
    s5o = g * jax.nn.sigmoid(_bdot(g, gw_ref[...]) + gb_ref[...])
    hy = jnp.where(is_ctx, hyc_ref[...], hyl_ref[...])
    at = jnp.where(is_ctx, atc_ref[...], atl_ref[...]).astype(F32).T
    o = (_bdot(hy, wo_ref[0:256, :]) + _bdot(s5o, wo_ref[256:512, :]) + _bdot(po_ref[...], wo_ref[512:768, :])
         + _bdot(at, wo_ref[768:1024, :]))
    x = x_ref[...] + m[2:3] * _rms(o, g1_ref[...])
    h = _rms(x, g2_ref[...]) * (1.0 + m[4:5]) + m[3:4]
    f = jnp.dot(h.astype(BF16), w1_ref[...], preferred_element_type=F32)
    f = jnp.square(jnp.maximum(f, 0.0)).astype(BF16)
    f = jnp.dot(f, w2_ref[...], preferred_element_type=F32)
    o_ref[...] = x + m[5:6] * _rms(f, g3_ref[...])


def _post(x, mods, hy_l, hy_c, s5y, po, at_l, at_c, glu_w, glu_b, w_out, g1, g2, g3, w1, w2):
    row = lambda w: pl.BlockSpec((TR, w), lambda i: (i, 0))
    lat = lambda i: jnp.minimum(i, NLB - 1)
    return pl.pallas_call(
        _post_kernel,
        grid=(NRB,),
        in_specs=[row(D),
                  pl.BlockSpec((1, N_MOD, D), lambda i: (i // NLB, 0, 0)),
                  pl.BlockSpec((TR, 256), lambda i: (lat(i), 0)),
                  _const_spec((LC, 256)),
                  row(256), row(256),
                  pl.BlockSpec((QH * HEAD, TR), lambda i: (0, lat(i))),
                  _const_spec((QH * HEAD, LC)),
                  _const_spec((256, 256)), _const_spec((1, 256)), _const_spec((D, D)),
                  _const_spec((1, D)), _const_spec((1, D)), _const_spec((1, D)),
                  _const_spec((D, D_FF)), _const_spec((D_FF, D))],
        out_specs=row(D),
        out_shape=jax.ShapeDtypeStruct((LT, D), F32),
        compiler_params=_params("parallel"),
        name="post_mix_mlp",
    )(x, mods, hy_l, hy_c, s5y, po, at_l, at_c, glu_w, glu_b, w_out, g1, g2, g3, w1, w2)


def _dft_tables():
    n = jnp.arange(128, dtype=jnp.int32)
    m = ((2 * n[:, None] + 1) * n[None, :]) % (2 * FFT_N1)
    ang = m.astype(F32) * (2.0 * math.pi / (2 * FFT_N1))
    c1, s1 = jnp.cos(ang), jnp.sin(ang)
    m1 = jnp.concatenate([c1, -s1], axis=0)
    m1inv = jnp.concatenate([c1.T, -s1.T], axis=1) * (2.0 / FFT_N)
    kk = n[:, None, None] + FFT_N1 * n[None, :, None]
    mm = ((2 * kk + 1) * n[None, None, :]) % (2 * FFT_N)
    phi = mm.astype(F32) * (2.0 * math.pi / (2 * FFT_N))
    cm, sm = jnp.cos(phi), jnp.sin(phi)
    w = jnp.concatenate([jnp.concatenate([cm, sm], axis=2), jnp.concatenate([-sm, cm], axis=2)], axis=1)
    wt = jnp.swapaxes(w, 1, 2)
    nc = jnp.arange(LC, dtype=jnp.int32)
    mc = ((2 * nc[:, None] + 1) * nc[None, :]) % (4 * LC)
    angc = mc.astype(F32) * (2.0 * math.pi / (4 * LC))
    cc, sc = jnp.cos(angc), jnp.sin(angc)
    fc = jnp.concatenate([cc, -sc], axis=0)
    fi = jnp.concatenate([cc.T, -sc.T], axis=1) * (2.0 / (2 * LC))
    return m1.astype(BF16), m1inv.astype(BF16), w.astype(BF16), wt.astype(BF16), fc, fi


def _rope_tables():
    t = jnp.arange(L, dtype=jnp.int32)
    inv = ROPE_THETA ** (-jnp.arange(0, HEAD // 2, 2, dtype=F32) / (HEAD // 2))
    ang = jnp.concatenate([(t // GRID_W).astype(F32)[:, None] * inv[None, :],
                           (t % GRID_W).astype(F32)[:, None] * inv[None, :]], axis=-1)
    ang = jnp.concatenate([ang, jnp.zeros((LC, HEAD // 2), F32)], axis=0)
    c, s = jnp.cos(ang), jnp.sin(ang)
    return jnp.concatenate([c, c, c, c], axis=1), jnp.concatenate([-s, s, -s, s], axis=1)


def _s5_tables(a_re, a_im, log_dt, b_re, b_im, c_re, c_im, d):
    dt = jnp.exp(log_dt)[..., None]
    tau = jnp.arange(S5_T + 1, dtype=F32)[:, None, None, None]
    mag = jnp.exp(a_re * dt * tau)
    pr, pi = mag * jnp.cos(a_im * dt * tau), mag * jnp.sin(a_im * dt * tau)
    lam_re, lam_im = pr[1], pi[1]
    den = a_re * a_re + a_im * a_im
    nr, ni = lam_re - 1.0, lam_im
    cr = (nr * a_re + ni * a_im) / den
    ci = (ni * a_re - nr * a_im) / den
    bb_re = cr[..., None] * b_re - ci[..., None] * b_im
    bb_im = cr[..., None] * b_im + ci[..., None] * b_re
    cl_re = c_re[None] * pr[:, :, :, None, :] - c_im[None] * pi[:, :, :, None, :]
    cl_im = c_re[None] * pi[:, :, :, None, :] + c_im[None] * pr[:, :, :, None, :]
    kk = jnp.einsum('tdghp,dgpk->tdghk', cl_re, bb_re) - jnp.einsum('tdghp,dgpk->tdghk', cl_im, bb_im)
    k0 = kk[0, 0] + kk[0, 1] + d.reshape(S5_G, S5_H)[:, :, None] * jnp.eye(S5_H, dtype=F32)[None]
    kfull = jnp.concatenate([kk[1:S5_T, 1][::-1], k0[None], kk[1:S5_T, 0]], axis=0)
    st = jnp.arange(S5_T)
    kt = kfull[st[None, :] - st[:, None] + S5_T - 1]
    th = S5_T * S5_H
    expand = jnp.einsum('tu,hk->thuk', jnp.eye(S5_T, dtype=F32), jnp.eye(S5_H, dtype=F32))
    expand = jnp.broadcast_to(expand[:, :, :, None, :], (S5_T, S5_H, S5_T, S5_G, S5_H)).reshape(th, S5_W)
    col_g = (jnp.arange(S5_W) // S5_H) % S5_G
    row_g_u = (jnp.arange(S5_W) // S5_H) % S5_G
    row_g_st = (jnp.arange(S5_W) // S5_P) % S5_G
    st_col_g = jnp.arange(S5_G * S5_P) // S5_P

    def widen(compact, row_g):
        full = jnp.dot(compact, expand, precision=lax.Precision.HIGHEST)
        return jnp.where(row_g[:, None] == col_g[None, :], full, 0.0).astype(BF16)

    m_intra = widen(kt.transpose(0, 2, 4, 1, 3).reshape(S5_W, th), row_g_u)

    def state_in(pw_re, pw_im, d_):
        re = pw_re[:, :, :, None] * bb_re[d_][None] - pw_im[:, :, :, None] * bb_im[d_][None]
        im = pw_re[:, :, :, None] * bb_im[d_][None] + pw_im[:, :, :, None] * bb_re[d_][None]

        def wide(x):
            c = x.transpose(0, 1, 3, 2).reshape(S5_W, S5_P)
            return jnp.where(row_g_u[:, None] == st_col_g[None, :], jnp.tile(c, (1, S5_G)), 0.0)
        return jnp.concatenate([wide(re), wide(im)], axis=1).astype(BF16)

    m_fst = state_in(pr[:S5_T, 0][::-1], pi[:S5_T, 0][::-1], 0)
    m_bst = state_in(pr[:S5_T, 1], pi[:S5_T, 1], 1)

    def state_out(cre, cim):
        both = jnp.stack([cre, -cim], axis=0)
        return widen(both.transpose(0, 2, 4, 1, 3).reshape(S5_W, th), row_g_st)

    m_fout = state_out(cl_re[1:, 0], cl_im[1:, 0])
    m_bout = state_out(cl_re[1:, 1][::-1], cl_im[1:, 1][::-1])
    lam8 = jnp.concatenate([pr[S5_T].reshape(2, -1), pi[S5_T].reshape(2, -1)], axis=1)
    return jnp.stack([m_intra, m_fst, m_bst], axis=0), lam8, m_fout, m_bout


def _filter_params(w1, b1, w2, b2, w3, decay):
    w1p = jnp.concatenate([w1, jnp.zeros((128 - w1.shape[0], HY_FFN), F32)], axis=0)
    w3r = w3.reshape(HY_FFN, HY_ORDER, 2, DG)
    return (w1p, b1.reshape(1, -1), w2, b2.reshape(1, -1),
            w3r[:, :, 0].reshape(HY_FFN, 512), w3r[:, :, 1].reshape(HY_FFN, 512),
            decay[:, 0].reshape(1, 512), decay[:, 1].reshape(1, 512))


def _block_diag(w):
    g, n, _ = w.shape
    return jnp.einsum('gcd,gj->gcjd', w, jnp.eye(g, dtype=w.dtype)).reshape(g * n, g * n)


def kernel(x, c, ctx, c_ctx, mod_w, mod_b, norm_pre_mix, norm_post_mix, norm_pre_mlp, norm_post_mlp,
           w_in, w_out, hy_conv_w, hy_conv_b, hy_ffn_w1, hy_ffn_b1, hy_ffn_w2, hy_ffn_b2, hy_ffn_w3,
           hy_decay, hy_bias, s5_a_re, s5_a_im, s5_log_dt, s5_b_re, s5_b_im, s5_c_re, s5_c_im, s5_d,
           s5_glu_w, s5_glu_b, pool_w, pool_scale, att_q_norm, att_k_norm, mlp_w1, mlp_w2):
    xs = jnp.concatenate([x[0], ctx[0]], axis=0)
    mods = _modulation(c, c_ctx, mod_w, mod_b)
    m1, m1inv, wk, wkt, fc, fi = _dft_tables()
    cs, sn = _rope_tables()
    perm = jnp.concatenate([jnp.arange(0, HEAD, 2), jnp.arange(1, HEAD, 2)])
    qcols = 1280 + (jnp.arange(QH)[:, None] * HEAD + perm[None, :]).reshape(-1)
    kcols = 1536 + (jnp.arange(KVH)[:, None] * HEAD + perm[None, :]).reshape(-1)
    cols = jnp.concatenate([jnp.arange(1280), qcols, kcols, jnp.arange(1664, IN_COLS)])
    head_mean = _block_diag(jnp.full((QH, HEAD, HEAD), 1.0 / HEAD, F32)).astype(BF16)

    for i in range(DEPTH):
        w_in_i = w_in[i][:, cols].astype(BF16)
        qg = jnp.tile(att_q_norm[i][perm], QH).reshape(1, 256)
        kg = jnp.tile(att_k_norm[i][perm], KVH).reshape(1, 128)
        a, s, p, q, k, v = _premix(xs, mods[i], norm_pre_mix[i].reshape(1, D), w_in_i, cs, sn, qg, kg, head_mean)

        vv, x1, x2, po = _local(a, p, hy_conv_w[i], hy_conv_b[i].reshape(1, 768),
                                _block_diag(pool_w[i]), pool_scale[i].reshape(1, 256))

        fp = _filter_params(hy_ffn_w1[i], hy_ffn_b1[i], hy_ffn_w2[i], hy_ffn_b2[i], hy_ffn_w3[i], hy_decay[i])
        filt = _filter_gen(fp)
        fa = _fft_fwd1(filt.reshape(FFT_R, FFT_N2, 1024), m1)
        kf = _filt_spec(fa, wk)
        time_major = lambda t: t.reshape(LT // FFT_N2, FFT_N2, 256)
        z = time_major(vv)
        gates = (time_major(x1), time_major(x2))
        for o in range(HY_ORDER):
            fa_z = _fft_fwd1(z, m1)
            bm = _fft_mid(fa_z, wk, wkt, kf, o)
            z = _fft_inv1(bm, m1inv, gates[o], z, hy_bias[i][o].reshape(1, 256))
        hy_l = z.reshape(L, 256)
        hy_c = _ctx_hyena(a, hy_conv_w[i], hy_conv_b[i].reshape(1, 768), fp, hy_bias[i], fc, fi)

        w3, lam8, m_fout, m_bout = _s5_tables(s5_a_re[i], s5_a_im[i], s5_log_dt[i], s5_b_re[i], s5_b_im[i],
                                              s5_c_re[i], s5_c_im[i], s5_d[i])
        fb = _s5_in(s.reshape(S5_ROWS, S5_T, DG), w3)
        hf, hb = _s5_scan(fb, lam8)
        s5y = _s5_out(fb, hf, hb, m_fout, m_bout).reshape(LT, 256)

        at_l = _flash(q, k, v)
        at_c = _ctx_attn(q, k, v)

        xs = _post(xs, mods[i], hy_l, hy_c, s5y, po, at_l, at_c,
                   s5_glu_w[i].astype(BF16), s5_glu_b[i].reshape(1, 256), w_out[i].astype(BF16),
                   norm_post_mix[i].reshape(1, D), norm_pre_mlp[i].reshape(1, D), norm_post_mlp[i].reshape(1, D),
                   mlp_w1[i].astype(BF16), mlp_w2[i].astype(BF16))
    return xs[:L][None]
```

```python
import functools
import math

import jax
import jax.numpy as jnp
from jax import lax
from jax.experimental import pallas as pl
from jax.experimental.pallas import tpu as pltpu

F32 = jnp.float32
BF16 = jnp.bfloat16

D = 1024
L = 16384
LC = 256
LT = L + LC
DEPTH = 4
GRID_W = 64
EPS = 1e-6
N_MOD = 6
DG = 256
HY_ORDER = 2
HY_FREQS = 16
HY_FFN = 64
HY_SIN_FREQ = 1.0
S5_H = 16
S5_G = 16
S5_P = 64
HEAD = 64
QH = 4
KVH = 2
ATT_SCALE = 1.0 / math.sqrt(HEAD)
ROPE_THETA = 10000.0
D_FF = 4 * D
IN_COLS = 1792

TR = 256
NLB = L // TR
NRB = LT // TR
HALO = 8

FFT_N = 2 * L
FFT_N1 = 256
FFT_N2 = 128
FFT_K1 = FFT_N1 // 2
FFT_R = L // FFT_N2
FFT_TJ = 16
FFT_TK = 8

S5_T = 8
S5_W = S5_T * DG
S5_ROWS = LT // S5_T
S5_SB = LC // S5_T
S5_NSB = S5_ROWS // S5_SB
S5_RB = 416

ATT_TQ = 1024
ATT_TK = 4096
ATT_KS = 4096
ATT_GQ = 512
ATT_NG = 2 * ATT_TQ // ATT_GQ
ATT_VE = HEAD + 16
LOG2E = math.log2(math.e)

VMEM_LIMIT = 56 * 1024 * 1024


def _params(*sem):
    return pltpu.CompilerParams(dimension_semantics=sem, vmem_limit_bytes=VMEM_LIMIT)


def _const_spec(shape):
    nd = len(shape)
    return pl.BlockSpec(shape, lambda *_: (0,) * nd, pipeline_mode=pl.Buffered(1))


def _bdot(a, b):
    return jnp.dot(a.astype(BF16), b.astype(BF16), preferred_element_type=F32)


def _split(a):
    hi = a.astype(BF16)
    lo = (a - hi.astype(F32)).astype(BF16)
    return hi, lo


def _dot3(a, b):
    ah, al = _split(a)
    bh, bl = _split(b)
    return (jnp.dot(ah, bh, preferred_element_type=F32)
            + jnp.dot(al, bh, preferred_element_type=F32)
            + jnp.dot(ah, bl, preferred_element_type=F32))


def _rms(x, g):
    return x * lax.rsqrt(jnp.mean(x * x, axis=-1, keepdims=True) + EPS) * g


def _mod_kernel(c_ref, w_ref, b_ref, o_ref):
    c = c_ref[...]
    s = c * jax.nn.sigmoid(c)
    o_ref[0] = _dot3(s, w_ref[0]) + b_ref[0]


def _modulation(c, c_ctx, mod_w, mod_b):
    cc = jnp.concatenate([c.reshape(1, D), c_ctx.reshape(1, D), jnp.zeros((6, D), F32)], axis=0)
    tn = 1536
    out = pl.pallas_call(
        _mod_kernel,
        grid=(DEPTH, N_MOD * D // tn),
        in_specs=[pl.BlockSpec((8, D), lambda i, j: (0, 0)),
                  pl.BlockSpec((1, D, tn), lambda i, j: (i, 0, j)),
                  pl.BlockSpec((1, 1, tn), lambda i, j: (i, 0, j))],
        out_specs=pl.BlockSpec((1, 8, tn), lambda i, j: (i, 0, j)),
        out_shape=jax.ShapeDtypeStruct((DEPTH, 8, N_MOD * D), F32),
        compiler_params=_params("parallel", "parallel"),
        name="modulation",
    )(cc, mod_w, mod_b.reshape(DEPTH, 1, N_MOD * D))
    return out[:, :2].reshape(DEPTH, 2, N_MOD, D)


def _qknorm_rope(t, gain, cs, sn, hm):
    hi, lo = _split(t * t)
    ms = jnp.dot(hi, hm, preferred_element_type=F32) + jnp.dot(lo, hm, preferred_element_type=F32)
    tn = t * lax.rsqrt(ms + EPS) * gain
    w = t.shape[1]
    lane = lax.broadcasted_iota(jnp.int32, tn.shape, 1)
    swapped = jnp.where((lane % HEAD) < HEAD // 2, pltpu.roll(tn, w - HEAD // 2, 1), pltpu.roll(tn, HEAD // 2, 1))
    return tn * cs + swapped * sn


def _premix_kernel(x_ref, mod_ref, g_ref, w_ref, cs_ref, sn_ref, qg_ref, kg_ref, hm_ref,
                   a_ref, s_ref, p_ref, q_ref, k_ref, v_ref):
    m = mod_ref[0]
    h = _rms(x_ref[...], g_ref[...]) * (1.0 + m[1:2]) + m[0:1]
    u = jnp.dot(h.astype(BF16), w_ref[...], preferred_element_type=F32)
    a_ref[...] = u[:, 0:768]
    s_ref[...] = u[:, 768:1024]
    p_ref[...] = u[:, 1024:1280]
    cs = cs_ref[...]
    sn = sn_ref[...]
    hm = hm_ref[...]
    q = _qknorm_rope(u[:, 1280:1536], qg_ref[...], jnp.concatenate([cs, cs], axis=1),
                     jnp.concatenate([sn, sn], axis=1), hm) * (ATT_SCALE * LOG2E)
    k = _qknorm_rope(u[:, 1536:1664], kg_ref[...], cs, sn, hm[:128, :128])
    q_ref[...] = q.T.astype(BF16)
    v_ref[...] = u[:, 1664:1792].T.astype(BF16)
    for hh in range(KVH):
        k_ref[hh] = k[:, HEAD * hh:HEAD * (hh + 1)].astype(BF16)


def _premix(x, mods, g, w_in, cs, sn, qg, kg, hm):
    row = lambda w: pl.BlockSpec((TR, w), lambda i: (i, 0))
    return pl.pallas_call(
        _premix_kernel,
        grid=(NRB,),
        in_specs=[row(D),
                  pl.BlockSpec((1, N_MOD, D), lambda i: (i // NLB, 0, 0)),
                  _const_spec((1, D)), _const_spec((D, IN_COLS)),
                  row(128), row(128),
                  _const_spec((1, 256)), _const_spec((1, 128)), _const_spec((256, 256))],
        out_specs=[row(768), row(256), row(256),
                   pl.BlockSpec((QH * HEAD, TR), lambda i: (0, i)),
                   pl.BlockSpec((KVH, TR, HEAD), lambda i: (0, i, 0)),
                   pl.BlockSpec((KVH * HEAD, TR), lambda i: (0, i))],
        out_shape=[jax.ShapeDtypeStruct((LT, 768), F32),
                   jax.ShapeDtypeStruct((LT, 256), F32),
                   jax.ShapeDtypeStruct((LT, 256), F32),
                   jax.ShapeDtypeStruct((QH * HEAD, LT), BF16),
                   jax.ShapeDtypeStruct((KVH, LT, HEAD), BF16),
                   jax.ShapeDtypeStruct((KVH * HEAD, LT), BF16)],
        compiler_params=_params("parallel"),
        name="premix",
    )(x, mods, g, w_in, cs, sn, qg, kg, hm)


def _local_kernel(a_ref, ap_ref, an_ref, p_ref, pp_ref, pn_ref, cw_ref, cb_ref, pw_ref, ps_ref,
                  v_ref, x1_ref, x2_ref, po_ref):
    i = pl.program_id(0)
    is_ctx = i >= NLB
    s0 = jnp.where(is_ctx, L, 0)
    s1 = jnp.where(is_ctx, LT, L)
    n = TR + 2 * HALO
    idx = i * TR - HALO + lax.broadcasted_iota(jnp.int32, (n, 1), 0)
    valid = (idx >= s0) & (idx < s1)

    def shifted(e, d):
        return pltpu.roll(e, (-d) % n, 0)[HALO:HALO + TR] if d else e[HALO:HALO + TR]

    ea = jnp.where(valid, jnp.concatenate([ap_ref[...], a_ref[...], an_ref[...]], axis=0), 0.0)
    w = cw_ref[...]
    y = cb_ref[...] + w[0:1] * shifted(ea, -1) + w[1:2] * shifted(ea, 0) + w[2:3] * shifted(ea, 1)
    v_ref[...] = y[:, 0:256]
    x1_ref[...] = y[:, 256:512]
    x2_ref[...] = y[:, 512:768]

    e = jnp.where(valid, jnp.concatenate([pp_ref[...], p_ref[...], pn_ref[...]], axis=0), 0.0)
    w2 = e + pltpu.roll(e, 1, 0)
    w4 = pltpu.roll(w2, 1, 0) + pltpu.roll(w2, n - 1, 0)
    w8 = pltpu.roll(w4, 2, 0) + pltpu.roll(w4, n - 2, 0)
    w16 = pltpu.roll(w8, 4, 0) + pltpu.roll(w8, n - 4, 0)
    lane = lax.broadcasted_iota(jnp.int32, (TR, 256), 1)
    cut = lambda t: t[HALO:HALO + TR]
    sums = jnp.where(lane < 64, cut(w2), jnp.where(lane < 128, cut(w4), jnp.where(lane < 192, cut(w8), cut(w16))))
    half = jnp.where(lane < 64, 1, jnp.where(lane < 128, 2, jnp.where(lane < 192, 4, 8)))
    tl = i * TR + lax.broadcasted_iota(jnp.int32, (TR, 256), 0) - s0
    cnt = jnp.minimum(tl + half, s1 - s0) - jnp.maximum(tl - half, 0)
    pooled = sums / cnt.astype(F32) - p_ref[...]
    po_ref[...] = _bdot(pooled, pw_ref[...]) * ps_ref[...]


def _local(a, p, conv_w, conv_b, pool_bd, pool_scale):
    r8 = TR // HALO
    main = lambda w: pl.BlockSpec((TR, w), lambda i: (i, 0))
    prev = lambda w: pl.BlockSpec((HALO, w), lambda i: (jnp.maximum(i * r8 - 1, 0), 0))
    nxt = lambda w: pl.BlockSpec((HALO, w), lambda i: (jnp.minimum((i + 1) * r8, LT // HALO - 1), 0))
    o = jax.ShapeDtypeStruct((LT, 256), F32)
    return pl.pallas_call(
        _local_kernel,
        grid=(NRB,),
        in_specs=[main(768), prev(768), nxt(768), main(256), prev(256), nxt(256),
                  _const_spec((3, 768)), _const_spec((1, 768)), _const_spec((256, 256)), _const_spec((1, 256))],
        out_specs=[main(256)] * 4,
        out_shape=[o, o, o, o],
        compiler_params=_params("parallel"),
        name="local_ops",
    )(a, a, a, p, p, p, conv_w, conv_b, pool_bd, pool_scale)


def _filter_hidden(n, inv_len, w1, b1, w2, b2):
    t = n * inv_len
    lane = lax.broadcasted_iota(jnp.int32, (n.shape[0], 128), 1)
    fr = jnp.where(lane <= HY_FREQS, lane, lane - HY_FREQS).astype(F32)
    ang = (2.0 * math.pi * t) * fr + jnp.where(lane <= HY_FREQS, 0.5 * math.pi, 0.0)
    feats = jnp.where(lane == 0, t, jnp.where(lane <= 2 * HY_FREQS, jnp.sin(ang), 0.0))
    h = jnp.sin(HY_SIN_FREQ * (_dot3(feats, w1) + b1))
    h = jnp.sin(HY_SIN_FREQ * (_dot3(h, w2) + b2))
    return h, t


def _filter_block(n0, rows, inv_len, w1, b1, w2, b2, w3f, w3b, dec_f, dec_b):
    n_ext = (n0 - HALO + lax.broadcasted_iota(jnp.int32, (rows + HALO, 1), 0)).astype(F32)
    h_ext, t_ext = _filter_hidden(n_ext, inv_len, w1, b1, w2, b2)
    n, t = n_ext[HALO:], t_ext[HALO:]
    hf = _dot3(h_ext[HALO:], w3f) * jnp.exp(-t * jnp.abs(dec_f))
    tp = t - inv_len
    hb = _dot3(pltpu.roll(h_ext, 1, 0)[HALO:], w3b) * jnp.exp(-tp * jnp.abs(dec_b))
    hb = jnp.where(n >= 1.0, hb, 0.0)
    return hf, hb


def _filter_kernel(w1_ref, b1_ref, w2_ref, b2_ref, w3f_ref, w3b_ref, df_ref, db_ref, o_ref, *, rows):
    hf, hb = _filter_block(pl.program_id(0) * rows, rows, 1.0 / L, w1_ref[...], b1_ref[...], w2_ref[...],
                           b2_ref[...], w3f_ref[...], w3b_ref[...], df_ref[...], db_ref[...])
    o_ref[:, 0:256] = hf[:, 0:256]
    o_ref[:, 256:512] = hb[:, 0:256]
    o_ref[:, 512:768] = hf[:, 256:512]
    o_ref[:, 768:1024] = hb[:, 256:512]


def _filter_gen(fp):
    rows = 512
    return pl.pallas_call(
        functools.partial(_filter_kernel, rows=rows),
        grid=(L // rows,),
        in_specs=[_const_spec((128, HY_FFN)), _const_spec((1, HY_FFN)), _const_spec((HY_FFN, HY_FFN)),
                  _const_spec((1, HY_FFN)), _const_spec((HY_FFN, 512)), _const_spec((HY_FFN, 512)),
                  _const_spec((1, 512)), _const_spec((1, 512))],
        out_specs=pl.BlockSpec((rows, 1024), lambda i: (i, 0)),
        out_shape=jax.ShapeDtypeStruct((L, 1024), F32),
        compiler_params=_params("parallel"),
        name="hyena_filter",
    )(*fp)


def _fft_fwd1_kernel(x_ref, m_ref, o_ref):
    x2 = x_ref.reshape(FFT_R * FFT_TJ, 128)
    o2 = o_ref.reshape(2 * FFT_K1 * FFT_TJ, 128)
    for jj in range(FFT_TJ):
        xj = x2[pl.ds(jj, FFT_R, stride=FFT_TJ), :]
        res = jnp.dot(m_ref[...], xj.astype(BF16), preferred_element_type=F32)
        o2[pl.ds(jj, FFT_K1, stride=FFT_TJ), :] = res[:FFT_K1]
        o2[pl.ds(FFT_K1 * FFT_TJ + jj, FFT_K1, stride=FFT_TJ), :] = res[FFT_K1:]


def _fft_fwd1(x3, m1):
    c = x3.shape[2]
    return pl.pallas_call(
        _fft_fwd1_kernel,
        grid=(FFT_N2 // FFT_TJ, c // 128),
        in_specs=[pl.BlockSpec((FFT_R, FFT_TJ, 128), lambda j, cc: (0, j, cc)), _const_spec((FFT_N1, FFT_R))],
        out_specs=pl.BlockSpec((2, FFT_K1, FFT_TJ, 128), lambda j, cc: (0, 0, j, cc)),
        out_shape=jax.ShapeDtypeStruct((2, FFT_K1, FFT_N2, c), F32),
        compiler_params=_params("parallel", "parallel"),
        name="fft_fwd1",
    )(x3, m1)


def _filt_spec_kernel(a_ref, w_ref, o_ref):
    for j in range(FFT_TK):
        a = jnp.concatenate([a_ref[0, j], a_ref[1, j]], axis=0).astype(BF16)
        x = jnp.dot(w_ref[j], a, preferred_element_type=F32)
        o_ref[0, 0, j] = x[:FFT_N2, 0:256] + x[:FFT_N2, 256:512]
        o_ref[0, 1, j] = x[FFT_N2:, 0:256] - x[FFT_N2:, 256:512]


def _filt_spec(fa, w):
    return pl.pallas_call(
        _filt_spec_kernel,
        grid=(HY_ORDER, FFT_K1 // FFT_TK),
        in_specs=[pl.BlockSpec((2, FFT_TK, FFT_N2, 512), lambda o, i: (0, i, 0, o)),
                  pl.BlockSpec((FFT_TK, 256, 256), lambda o, i: (i, 0, 0))],
        out_specs=pl.BlockSpec((1, 2, FFT_TK, FFT_N2, 256), lambda o, i: (o, 0, i, 0, 0)),
        out_shape=jax.ShapeDtypeStruct((HY_ORDER, 2, FFT_K1, FFT_N2, 256), F32),
        compiler_params=_params("parallel", "parallel"),
        name="hyena_filter_spectrum",
    )(fa, w)


def _fft_mid_kernel(a_ref, w_ref, wt_ref, k_ref, o_ref):
    for j in range(FFT_TK):
        a = jnp.concatenate([a_ref[0, j], a_ref[1, j]], axis=0).astype(BF16)
        x = jnp.dot(w_ref[j], a, preferred_element_type=F32)
        xr, xi = x[:FFT_N2], x[FFT_N2:]
        kr, ki = k_ref[0, 0, j], k_ref[0, 1, j]
        y = jnp.concatenate([xr * kr - xi * ki, xr * ki + xi * kr], axis=0).astype(BF16)
        b = jnp.dot(wt_ref[j], y, preferred_element_type=F32)
        o_ref[0, j] = b[:FFT_N2]
        o_ref[1, j] = b[FFT_N2:]


def _fft_mid(a, w, wt, kf, order):
    blk = pl.BlockSpec((2, FFT_TK, FFT_N2, 256), lambda i: (0, i, 0, 0))
    wblk = pl.BlockSpec((FFT_TK, 256, 256), lambda i: (i, 0, 0))
    return pl.pallas_call(
        _fft_mid_kernel,
        grid=(FFT_K1 // FFT_TK,),
        in_specs=[blk, wblk, wblk,
                  pl.BlockSpec((1, 2, FFT_TK, FFT_N2, 256), lambda i: (order, 0, i, 0, 0))],
        out_specs=blk,
        out_shape=jax.ShapeDtypeStruct((2, FFT_K1, FFT_N2, 256), F32),
        compiler_params=_params("parallel"),
        name="fft_mid",
    )(a, w, wt, kf)


def _fft_inv1_kernel(b_ref, m_ref, x_ref, z_ref, fb_ref, o_ref):
    b2 = b_ref.reshape(2 * FFT_K1 * FFT_TJ, 128)
    x2 = x_ref.reshape(FFT_R * FFT_TJ, 128)
    z2 = z_ref.reshape(FFT_R * FFT_TJ, 128)
    o2 = o_ref.reshape(FFT_R * FFT_TJ, 128)
    for jj in range(FFT_TJ):
        b = jnp.concatenate([b2[pl.ds(jj, FFT_K1, stride=FFT_TJ), :],
                             b2[pl.ds(FFT_K1 * FFT_TJ + jj, FFT_K1, stride=FFT_TJ), :]], axis=0).astype(BF16)
        y = jnp.dot(m_ref[...], b, preferred_element_type=F32)
        rows = pl.ds(jj, FFT_R, stride=FFT_TJ)
        o2[rows, :] = x2[rows, :] * (y + fb_ref[...] * z2[rows, :])


def _fft_inv1(b4, minv, gate3, z3, fb):
    sig = pl.BlockSpec((FFT_R, FFT_TJ, 128), lambda j, cc: (0, j, cc))
    return pl.pallas_call(
        _fft_inv1_kernel,
        grid=(FFT_N2 // FFT_TJ, 2),
        in_specs=[pl.BlockSpec((2, FFT_K1, FFT_TJ, 128), lambda j, cc: (0, 0, j, cc)),
                  _const_spec((FFT_R, FFT_N1)), sig, sig, pl.BlockSpec((1, 128), lambda j, cc: (0, cc))],
        out_specs=sig,
        out_shape=jax.ShapeDtypeStruct((FFT_R, FFT_N2, 256), F32),
        compiler_params=_params("parallel", "parallel"),
        name="fft_inv1_gate",
    )(b4, minv, gate3, z3, fb)


def _ctx_hyena_kernel(a_ref, cw_ref, cb_ref, w1_ref, b1_ref, w2_ref, b2_ref, w3f_ref, w3b_ref, df_ref, db_ref,
                      fb_ref, fc_ref, fi_ref, o_ref):
    zero = jnp.zeros((HALO, 768), F32)
    n = LC + 2 * HALO
    ea = jnp.concatenate([zero, a_ref[...], zero], axis=0)
    w = cw_ref[...]
    y = (cb_ref[...] + w[0:1] * pltpu.roll(ea, 1, 0)[HALO:HALO + LC] + w[1:2] * ea[HALO:HALO + LC]
         + w[2:3] * pltpu.roll(ea, n - 1, 0)[HALO:HALO + LC])
    hf, hb = _filter_block(0, LC, 1.0 / LC, w1_ref[...], b1_ref[...], w2_ref[...], b2_ref[...],
                           w3f_ref[...], w3b_ref[...], df_ref[...], db_ref[...])
    fc = fc_ref[...]
    fi = fi_ref[...]
    sf = _dot3(fc, hf)
    sb = _dot3(fc, hb)
    kr = sf[:LC] + sb[:LC]
    ki = sf[LC:] - sb[LC:]
    z = y[:, 0:256]
    fb = fb_ref[...]
    for o in range(HY_ORDER):
        s = _dot3(fc, z)
        sr, si = s[:LC], s[LC:]
        c0 = 256 * o
        yr = sr * kr[:, c0:c0 + 256] - si * ki[:, c0:c0 + 256]
        yi = sr * ki[:, c0:c0 + 256] + si * kr[:, c0:c0 + 256]
        conv = _dot3(fi, jnp.concatenate([yr, yi], axis=0))
        z = y[:, 256 * (o + 1):256 * (o + 2)] * (conv + fb[o:o + 1] * z)
    o_ref[...] = z


def _ctx_hyena(a, conv_w, conv_b, fp, fbias, fc, fi):
    return pl.pallas_call(
        _ctx_hyena_kernel,
        grid=(1,),
        in_specs=[pl.BlockSpec((LC, 768), lambda i: (L // LC, 0)),
                  _const_spec((3, 768)), _const_spec((1, 768)),
                  _const_spec((128, HY_FFN)), _const_spec((1, HY_FFN)), _const_spec((HY_FFN, HY_FFN)),
                  _const_spec((1, HY_FFN)), _const_spec((HY_FFN, 512)), _const_spec((HY_FFN, 512)),
                  _const_spec((1, 512)), _const_spec((1, 512)),
                  _const_spec((HY_ORDER, 256)), _const_spec((2 * LC, LC)), _const_spec((LC, 2 * LC))],
        out_specs=pl.BlockSpec((LC, 256), lambda i: (0, 0)),
        out_shape=jax.ShapeDtypeStruct((LC, 256), F32),
        compiler_params=_params("arbitrary"),
        name="hyena_ctx",
    )(a, conv_w, conv_b, *fp, fbias, fc, fi)


def _s5_in_kernel(ua_ref, ub_ref, w_ref, o_ref):
    u = jnp.concatenate([r[pl.ds(j, S5_RB, stride=S5_T), :].astype(BF16)
                         for j in range(S5_T) for r in (ua_ref, ub_ref)], axis=1)
    o_ref[0] = jnp.dot(u, w_ref[0], preferred_element_type=F32)


def _s5_in(u2, w3):
    return pl.pallas_call(
        _s5_in_kernel,
        grid=(3, S5_ROWS // S5_RB),
        in_specs=[pl.BlockSpec((S5_RB * S5_T, DG // 2), lambda m, r: (r, 0)),
                  pl.BlockSpec((S5_RB * S5_T, DG // 2), lambda m, r: (r, 1)),
                  pl.BlockSpec((1, S5_W, S5_W), lambda m, r: (m, 0, 0))],
        out_specs=pl.BlockSpec((1, S5_RB, S5_W), lambda m, r: (m, r, 0)),
        out_shape=jax.ShapeDtypeStruct((3, S5_ROWS, S5_W), F32),
        compiler_params=_params("parallel", "parallel"),
        name="s5_chunk_in",
    )(u2, u2, w3)


def _s5_scan_kernel(f_ref, b_ref, lam_ref, hf_ref, hb_ref, sf, sb, bf, bb):
    j = pl.program_id(0)

    @pl.when(j == 0)
    def _():
        sf[...] = jnp.zeros_like(sf)
        sb[...] = jnp.zeros_like(sb)

    half = S5_W // 2
    lam = lam_ref[...]
    lfr, lfi = lam[0:1, :half], lam[0:1, half:]
    lbr, lbi = lam[1:2, :half], lam[1:2, half:]

    def cmul(lr, li, h):
        hr, hi = h[:, :half], h[:, half:]
        return jnp.concatenate([lr * hr - li * hi, lr * hi + li * hr], axis=1)

    def body(r, carry):
        hf, hb = carry
        bf[pl.ds(r, 1), :] = hf
        hf = cmul(lfr, lfi, hf) + f_ref[0, pl.ds(r, 1), :]
        rb = S5_SB - 1 - r
        bb[pl.ds(rb, 1), :] = hb
        hb = cmul(lbr, lbi, hb) + b_ref[0, pl.ds(rb, 1), :]
        return hf, hb

    hf, hb = lax.fori_loop(0, S5_SB, body, (sf[...], sb[...]))
    sf[...] = hf
    sb[...] = hb
    hf_ref[...] = bf[...].astype(BF16)
    hb_ref[...] = bb[...].astype(BF16)


def _s5_scan(fb, lam8):
    fidx = lambda j: jnp.where(j == 0, S5_NSB - 1, j - 1)
    bidx = lambda j: S5_NSB - 1 - j
    o = jax.ShapeDtypeStruct((S5_ROWS, S5_W), BF16)
    return pl.pallas_call(
        _s5_scan_kernel,
        grid=(S5_NSB,),
        in_specs=[pl.BlockSpec((1, S5_SB, S5_W), lambda j: (1, fidx(j), 0)),
                  pl.BlockSpec((1, S5_SB, S5_W), lambda j: (2, bidx(j), 0)),
                  _const_spec((2, S5_W))],
        out_specs=[pl.BlockSpec((S5_SB, S5_W), lambda j: (fidx(j), 0)),
                   pl.BlockSpec((S5_SB, S5_W), lambda j: (bidx(j), 0))],
        out_shape=[o, o],
        scratch_shapes=[pltpu.VMEM((1, S5_W), F32), pltpu.VMEM((1, S5_W), F32),
                        pltpu.VMEM((S5_SB, S5_W), F32), pltpu.VMEM((S5_SB, S5_W), F32)],
        compiler_params=_params("arbitrary"),
        name="s5_scan",
    )(fb, fb, lam8)


def _s5_out_kernel(y_ref, hf_ref, hb_ref, wf_ref, wb_ref, o_ref, y_sc):
    half = pl.program_id(1)

    @pl.when(half == 0)
    def _():
        y_sc[...] = (y_ref[0] + jnp.dot(hf_ref[...], wf_ref[...], preferred_element_type=F32)
                     + jnp.dot(hb_ref[...], wb_ref[...], preferred_element_type=F32))

    for hh in range(2):
        @pl.when(half == hh)
        def _():
            for j in range(S5_T):
                c0 = DG * j + 128 * hh
                o_ref[pl.ds(j, S5_RB, stride=S5_T), :] = y_sc[:, c0:c0 + 128]


def _s5_out(fb, hf, hb, wf, wb):
    rows = pl.BlockSpec((S5_RB, S5_W), lambda r, c: (r, 0))
    return pl.pallas_call(
        _s5_out_kernel,
        grid=(S5_ROWS // S5_RB, 2),
        in_specs=[pl.BlockSpec((1, S5_RB, S5_W), lambda r, c: (0, r, 0)), rows, rows,
                  _const_spec((S5_W, S5_W)), _const_spec((S5_W, S5_W))],
        out_specs=pl.BlockSpec((S5_RB * S5_T, 128), lambda r, c: (r, c)),
        out_shape=jax.ShapeDtypeStruct((LT, DG), F32),
        scratch_shapes=[pltpu.VMEM((S5_RB, S5_W), F32)],
        compiler_params=_params("parallel", "arbitrary"),
        name="s5_chunk_out",
    )(fb, hf, hb, wf, wb)


def _value_rows(vt):
    return jnp.concatenate([vt, jnp.ones((ATT_VE - HEAD, vt.shape[1]), BF16)], axis=0)


def _attend(k, ve, q_ref, m_sc, acc_sc):
    ks = min(ATT_KS, k.shape[0])
    units = [(c, g) for c in range(k.shape[0] // ks) for g in range(ATT_NG)]

    def scores(unit):
        c, g = unit
        j, c0 = divmod(g * ATT_GQ, ATT_TQ)
        return jnp.dot(k[c * ks:(c + 1) * ks], q_ref[HEAD * j:HEAD * (j + 1), c0:c0 + ATT_GQ],
                       preferred_element_type=F32)

    s_next = scores(units[0])
    for u, (c, g) in enumerate(units):
        s = s_next
        if u + 1 < len(units):
            s_next = scores(units[u + 1])
        m_prev = m_sc[g]
        m_new = jnp.maximum(m_prev, jnp.max(s, axis=0, keepdims=True))
        alpha = jnp.exp2(m_prev - m_new)
        p = jnp.exp2(s - m_new).astype(BF16)
        acc_sc[g] = alpha * acc_sc[g] + jnp.dot(ve[:, c * ks:(c + 1) * ks], p, preferred_element_type=F32)
        m_sc[g] = m_new


def _flash_kernel(q_ref, k_ref, v_ref, kc_ref, vc_ref, o_ref, m_sc, acc_sc):
    ki = pl.program_id(2)

    @pl.when(ki == 0)
    def _():
        m_sc[...] = jnp.full_like(m_sc, -1e30)
        acc_sc[...] = jnp.zeros_like(acc_sc)

    _attend(k_ref[0], _value_rows(v_ref[...]), q_ref, m_sc, acc_sc)

    @pl.when(ki == pl.num_programs(2) - 1)
    def _():
        _attend(kc_ref[0], _value_rows(vc_ref[...]), q_ref, m_sc, acc_sc)
        for g in range(ATT_NG):
            j, c0 = divmod(g * ATT_GQ, ATT_TQ)
            acc = acc_sc[g]
            o_ref[HEAD * j:HEAD * (j + 1), c0:c0 + ATT_GQ] = (acc[:HEAD] / acc[HEAD:HEAD + 1]).astype(BF16)


def _flash(qt, k, vt):
    return pl.pallas_call(
        _flash_kernel,
        grid=(KVH, L // ATT_TQ, L // ATT_TK),
        in_specs=[pl.BlockSpec((2 * HEAD, ATT_TQ), lambda h, qi, ki: (h, qi)),
                  pl.BlockSpec((1, ATT_TK, HEAD), lambda h, qi, ki: (h, ki, 0)),
                  pl.BlockSpec((HEAD, ATT_TK), lambda h, qi, ki: (h, ki)),
                  pl.BlockSpec((1, LC, HEAD), lambda h, qi, ki: (h, L // LC, 0)),
                  pl.BlockSpec((HEAD, LC), lambda h, qi, ki: (h, L // LC))],
        out_specs=pl.BlockSpec((2 * HEAD, ATT_TQ), lambda h, qi, ki: (h, qi)),
        out_shape=jax.ShapeDtypeStruct((QH * HEAD, L), BF16),
        scratch_shapes=[pltpu.VMEM((ATT_NG, 1, ATT_GQ), F32), pltpu.VMEM((ATT_NG, ATT_VE, ATT_GQ), F32)],
        compiler_params=_params("parallel", "parallel", "arbitrary"),
        name="flash_attention",
    )(qt, k, vt, k, vt)


def _ctx_attn_kernel(q_ref, k_ref, v_ref, o_ref):
    ve = _value_rows(v_ref[...])
    for j in range(2):
        s = jnp.dot(k_ref[0], q_ref[HEAD * j:HEAD * (j + 1), :], preferred_element_type=F32)
        p = jnp.exp2(s - jnp.max(s, axis=0, keepdims=True)).astype(BF16)
        acc = jnp.dot(ve, p, preferred_element_type=F32)
        o_ref[HEAD * j:HEAD * (j + 1), :] = (acc[:HEAD] / acc[HEAD:HEAD + 1]).astype(BF16)


def _ctx_attn(qt, k, vt):
    return pl.pallas_call(
        _ctx_attn_kernel,
        grid=(KVH,),
        in_specs=[pl.BlockSpec((2 * HEAD, LC), lambda h: (h, L // LC)),
                  pl.BlockSpec((1, LC, HEAD), lambda h: (h, L // LC, 0)),
                  pl.BlockSpec((HEAD, LC), lambda h: (h, L // LC))],
        out_specs=pl.BlockSpec((2 * HEAD, LC), lambda h: (h, 0)),
        out_shape=jax.ShapeDtypeStruct((QH * HEAD, LC), BF16),
        compiler_params=_params("parallel"),
        name="ctx_attention",
    )(qt, k, vt)


def _gelu_tanh(x):
    return 0.5 * x * (1.0 + jnp.tanh(math.sqrt(2.0 / math.pi) * (x + 0.044715 * (x * x * x))))


def _post_kernel(x_ref, mod_ref, hyl_ref, hyc_ref, s5_ref, po_ref, atl_ref, atc_ref,
                 gw_ref, gb_ref, wo_ref, g1_ref, g2_ref, g3_ref, w1_ref, w2_ref, o_ref):
    is_ctx = pl.program_id(0) >= NLB
    m = mod_ref[0]
    g = _gelu_tanh(s5_ref[...])
    s5o = g * jax.nn.sigmoid(_bdot(g, gw_ref[...]) + gb_ref[...])
    hy = jnp.where(is_ctx, hyc_ref[...], hyl_ref[...])
    at = jnp.where(is_ctx, atc_ref[...], atl_ref[...]).astype(F32).T
    o = (_bdot(hy, wo_ref[0:256, :]) + _bdot(s5o, wo_ref[256:512, :]) + _bdot(po_ref[...], wo_ref[512:768, :])
         + _bdot(at, wo_ref[768:1024, :]))
    x = x_ref[...] + m[2:3] * _rms(o, g1_ref[...])
    h = _rms(x, g2_ref[...]) * (1.0 + m[4:5]) + m[3:4]
    f = jnp.dot(h.astype(BF16), w1_ref[...], preferred_element_type=F32)
    f = jnp.square(jnp.maximum(f, 0.0)).astype(BF16)
    f = jnp.dot(f, w2_ref[...], preferred_element_type=F32)
    o_ref[...] = x + m[5:6] * _rms(f, g3_ref[...])


def _post(x, mods, hy_l, hy_c, s5y, po, at_l, at_c, glu_w, glu_b, w_out, g1, g2, g3, w1, w2, n_blocks):
    row = lambda w: pl.BlockSpec((TR, w), lambda i: (i, 0))
    lat = lambda i: jnp.minimum(i, NLB - 1)
    return pl.pallas_call(
        _post_kernel,
        grid=(n_blocks,),
        in_specs=[row(D),
                  pl.BlockSpec((1, N_MOD, D), lambda i: (i // NLB, 0, 0)),
                  pl.BlockSpec((TR, 256), lambda i: (lat(i), 0)),
                  _const_spec((LC, 256)),
                  row(256), row(256),
                  pl.BlockSpec((QH * HEAD, TR), lambda i: (0, lat(i))),
                  _const_spec((QH * HEAD, LC)),
                  _const_spec((256, 256)), _const_spec((1, 256)), _const_spec((D, D)),
                  _const_spec((1, D)), _const_spec((1, D)), _const_spec((1, D)),
                  _const_spec((D, D_FF)), _const_spec((D_FF, D))],
        out_specs=row(D),
        out_shape=jax.ShapeDtypeStruct((n_blocks * TR, D), F32),
        compiler_params=_params("parallel"),
        name="post_mix_mlp",
    )(x, mods, hy_l, hy_c, s5y, po, at_l, at_c, glu_w, glu_b, w_out, g1, g2, g3, w1, w2)


def _dft_tables():
    n = jnp.arange(128, dtype=jnp.int32)
    m = ((2 * n[:, None] + 1) * n[None, :]) % (2 * FFT_N1)
    ang = m.astype(F32) * (2.0 * math.pi / (2 * FFT_N1))
    c1, s1 = jnp.cos(ang), jnp.sin(ang)
    m1 = jnp.concatenate([c1, -s1], axis=0)
    m1inv = jnp.concatenate([c1.T, -s1.T], axis=1) * (2.0 / FFT_N)
    kk = n[:, None, None] + FFT_N1 * n[None, :, None]
    mm = ((2 * kk + 1) * n[None, None, :]) % (2 * FFT_N)
    phi = mm.astype(F32) * (2.0 * math.pi / (2 * FFT_N))
    cm, sm = jnp.cos(phi), jnp.sin(phi)
    w = jnp.concatenate([jnp.concatenate([cm, sm], axis=2), jnp.concatenate([-sm, cm], axis=2)], axis=1)
    wt = jnp.swapaxes(w, 1, 2)
    nc = jnp.arange(LC, dtype=jnp.int32)
    mc = ((2 * nc[:, None] + 1) * nc[None, :]) % (4 * LC)
    angc = mc.astype(F32) * (2.0 * math.pi / (4 * LC))
    cc, sc = jnp.cos(angc), jnp.sin(angc)
    fc = jnp.concatenate([cc, -sc], axis=0)
    fi = jnp.concatenate([cc.T, -sc.T], axis=1) * (2.0 / (2 * LC))
    return m1.astype(BF16), m1inv.astype(BF16), w.astype(BF16), wt.astype(BF16), fc, fi


def _rope_tables():
    t = jnp.arange(L, dtype=jnp.int32)
    inv = ROPE_THETA ** (-jnp.arange(0, HEAD // 2, 2, dtype=F32) / (HEAD // 2))
    ang = jnp.concatenate([(t // GRID_W).astype(F32)[:, None] * inv[None, :],
                           (t % GRID_W).astype(F32)[:, None] * inv[None, :]], axis=-1)
    ang = jnp.concatenate([ang, jnp.zeros((LC, HEAD // 2), F32)], axis=0)
    c, s = jnp.cos(ang), jnp.sin(ang)
    return jnp.concatenate([c, c, c, c], axis=1), jnp.concatenate([-s, s, -s, s], axis=1)


def _s5_tables(a_re, a_im, log_dt, b_re, b_im, c_re, c_im, d):
    dt = jnp.exp(log_dt)[..., None]
    tau = jnp.arange(S5_T + 1, dtype=F32)[:, None, None, None]
    mag = jnp.exp(a_re * dt * tau)
    pr, pi = mag * jnp.cos(a_im * dt * tau), mag * jnp.sin(a_im * dt * tau)
    lam_re, lam_im = pr[1], pi[1]
    den = a_re * a_re + a_im * a_im
    nr, ni = lam_re - 1.0, lam_im
    cr = (nr * a_re + ni * a_im) / den
    ci = (ni * a_re - nr * a_im) / den
    bb_re = cr[..., None] * b_re - ci[..., None] * b_im
    bb_im = cr[..., None] * b_im + ci[..., None] * b_re
    cl_re = c_re[None] * pr[:, :, :, None, :] - c_im[None] * pi[:, :, :, None, :]
    cl_im = c_re[None] * pi[:, :, :, None, :] + c_im[None] * pr[:, :, :, None, :]
    kk = jnp.einsum('tdghp,dgpk->tdghk', cl_re, bb_re) - jnp.einsum('tdghp,dgpk->tdghk', cl_im, bb_im)
    k0 = kk[0, 0] + kk[0, 1] + d.reshape(S5_G, S5_H)[:, :, None] * jnp.eye(S5_H, dtype=F32)[None]
    kfull = jnp.concatenate([kk[1:S5_T, 1][::-1], k0[None], kk[1:S5_T, 0]], axis=0)
    st = jnp.arange(S5_T)
    kt = kfull[st[None, :] - st[:, None] + S5_T - 1]
    th = S5_T * S5_H
    expand = jnp.einsum('tu,hk->thuk', jnp.eye(S5_T, dtype=F32), jnp.eye(S5_H, dtype=F32))
    expand = jnp.broadcast_to(expand[:, :, :, None, :], (S5_T, S5_H, S5_T, S5_G, S5_H)).reshape(th, S5_W)
    col_g = (jnp.arange(S5_W) // S5_H) % S5_G
    row_g_u = (jnp.arange(S5_W) // S5_H) % S5_G
    row_g_st = (jnp.arange(S5_W) // S5_P) % S5_G
    st_col_g = jnp.arange(S5_G * S5_P) // S5_P

    def widen(compact, row_g):
        full = jnp.dot(compact.astype(BF16), expand.astype(BF16), preferred_element_type=F32)
        return jnp.where(row_g[:, None] == col_g[None, :], full, 0.0).astype(BF16)

    m_intra = widen(kt.transpose(0, 2, 4, 1, 3).reshape(S5_W, th), row_g_u)

    def state_in(pw_re, pw_im, d_):
        re = pw_re[:, :, :, None] * bb_re[d_][None] - pw_im[:, :, :, None] * bb_im[d_][None]
        im = pw_re[:, :, :, None] * bb_im[d_][None] + pw_im[:, :, :, None] * bb_re[d_][None]

        def wide(x):
            c = x.transpose(0, 1, 3, 2).reshape(S5_W, S5_P)
            return jnp.where(row_g_u[:, None] == st_col_g[None, :], jnp.tile(c, (1, S5_G)), 0.0)
        return jnp.concatenate([wide(re), wide(im)], axis=1).astype(BF16)

    m_fst = state_in(pr[:S5_T, 0][::-1], pi[:S5_T, 0][::-1], 0)
    m_bst = state_in(pr[:S5_T, 1], pi[:S5_T, 1], 1)

    def state_out(cre, cim):
        both = jnp.stack([cre, -cim], axis=0)
        return widen(both.transpose(0, 2, 4, 1, 3).reshape(S5_W, th), row_g_st)

    m_fout = state_out(cl_re[1:, 0], cl_im[1:, 0])
    m_bout = state_out(cl_re[1:, 1][::-1], cl_im[1:, 1][::-1])
    lam8 = jnp.concatenate([pr[S5_T].reshape(2, -1), pi[S5_T].reshape(2, -1)], axis=1)
    return jnp.stack([m_intra, m_fst, m_bst], axis=0), lam8, m_fout, m_bout


def _filter_params(w1, b1, w2, b2, w3, decay):
    w1p = jnp.concatenate([w1, jnp.zeros((128 - w1.shape[0], HY_FFN), F32)], axis=0)
    w3r = w3.reshape(HY_FFN, HY_ORDER, 2, DG)
    return (w1p, b1.reshape(1, -1), w2, b2.reshape(1, -1),
            w3r[:, :, 0].reshape(HY_FFN, 512), w3r[:, :, 1].reshape(HY_FFN, 512),
            decay[:, 0].reshape(1, 512), decay[:, 1].reshape(1, 512))


def _block_diag(w):
    g, n, _ = w.shape
    return jnp.einsum('gcd,gj->gcjd', w, jnp.eye(g, dtype=w.dtype)).reshape(g * n, g * n)


def kernel(x, c, ctx, c_ctx, mod_w, mod_b, norm_pre_mix, norm_post_mix, norm_pre_mlp, norm_post_mlp,
           w_in, w_out, hy_conv_w, hy_conv_b, hy_ffn_w1, hy_ffn_b1, hy_ffn_w2, hy_ffn_b2, hy_ffn_w3,
           hy_decay, hy_bias, s5_a_re, s5_a_im, s5_log_dt, s5_b_re, s5_b_im, s5_c_re, s5_c_im, s5_d,
           s5_glu_w, s5_glu_b, pool_w, pool_scale, att_q_norm, att_k_norm, mlp_w1, mlp_w2):
    xs = jnp.concatenate([x[0], ctx[0]], axis=0)
    mods = _modulation(c, c_ctx, mod_w, mod_b)
    m1, m1inv, wk, wkt, fc, fi = _dft_tables()
    cs, sn = _rope_tables()
    perm = jnp.concatenate([jnp.arange(0, HEAD, 2), jnp.arange(1, HEAD, 2)])
    qcols = 1280 + (jnp.arange(QH)[:, None] * HEAD + perm[None, :]).reshape(-1)
    kcols = 1536 + (jnp.arange(KVH)[:, None] * HEAD + perm[None, :]).reshape(-1)
    cols = jnp.concatenate([jnp.arange(1280), qcols, kcols, jnp.arange(1664, IN_COLS)])
    head_mean = _block_diag(jnp.full((QH, HEAD, HEAD), 1.0 / HEAD, F32)).astype(BF16)

    for i in range(DEPTH):
        w_in_i = w_in[i][:, cols].astype(BF16)
        qg = jnp.tile(att_q_norm[i][perm], QH).reshape(1, 256)
        kg = jnp.tile(att_k_norm[i][perm], KVH).reshape(1, 128)
        a, s, p, q, k, v = _premix(xs, mods[i], norm_pre_mix[i].reshape(1, D), w_in_i, cs, sn, qg, kg, head_mean)

        vv, x1, x2, po = _local(a, p, hy_conv_w[i], hy_conv_b[i].reshape(1, 768),
                                _block_diag(pool_w[i]), pool_scale[i].reshape(1, 256))

        fp = _filter_params(hy_ffn_w1[i], hy_ffn_b1[i], hy_ffn_w2[i], hy_ffn_b2[i], hy_ffn_w3[i], hy_decay[i])
        filt = _filter_gen(fp)
        fa = _fft_fwd1(filt.reshape(FFT_R, FFT_N2, 1024), m1)
        kf = _filt_spec(fa, wk)
        time_major = lambda t: t.reshape(LT // FFT_N2, FFT_N2, 256)
        z = time_major(vv)
        gates = (time_major(x1), time_major(x2))
        for o in range(HY_ORDER):
            fa_z = _fft_fwd1(z, m1)
            bm = _fft_mid(fa_z, wk, wkt, kf, o)
            z = _fft_inv1(bm, m1inv, gates[o], z, hy_bias[i][o].reshape(1, 256))
        hy_l = z.reshape(L, 256)
        hy_c = _ctx_hyena(a, hy_conv_w[i], hy_conv_b[i].reshape(1, 768), fp, hy_bias[i], fc, fi)

        w3, lam8, m_fout, m_bout = _s5_tables(s5_a_re[i], s5_a_im[i], s5_log_dt[i], s5_b_re[i], s5_b_im[i],
                                              s5_c_re[i], s5_c_im[i], s5_d[i])
        fb = _s5_in(s, w3)
        hf, hb = _s5_scan(fb, lam8)
        s5y = _s5_out(fb, hf, hb, m_fout, m_bout)

        at_l = _flash(q, k, v)
        at_c = _ctx_attn(q, k, v)

        xs = _post(xs, mods[i], hy_l, hy_c, s5y, po, at_l, at_c,
                   s5_glu_w[i].astype(BF16), s5_glu_b[i].reshape(1, 256), w_out[i].astype(BF16),
                   norm_post_mix[i].reshape(1, D), norm_pre_mlp[i].reshape(1, D), norm_post_mlp[i].reshape(1, D),
                   mlp_w1[i].astype(BF16), mlp_w2[i].astype(BF16),
                   NRB if i < DEPTH - 1 else NLB)
    return xs[None]
```

```python
import functools
import math

import jax
import jax.numpy as jnp
from jax import lax
from jax.experimental import pallas as pl
from jax.experimental.pallas import tpu as pltpu

F32 = jnp.float32
BF16 = jnp.bfloat16

D = 1024
L = 16384
LC = 256
LT = L + LC
DEPTH = 4
GRID_W = 64
EPS = 1e-6
N_MOD = 6
DG = 256
HY_ORDER = 2
HY_FREQS = 16
HY_FFN = 64
HY_SIN_FREQ = 1.0
S5_H = 16
S5_G = 16
S5_P = 64
HEAD = 64
QH = 4
KVH = 2
ATT_SCALE = 1.0 / math.sqrt(HEAD)
ROPE_THETA = 10000.0
D_FF = 4 * D
IN_COLS = 1792

TR = 256
NLB = L // TR
NRB = LT // TR
HALO = 8

FFT_N = 2 * L
FFT_N1 = 256
FFT_N2 = 128
FFT_K1 = FFT_N1 // 2
FFT_R = L // FFT_N2
FFT_TJ = 16
FFT_TK = 8

S5_T = 8
S5_W = S5_T * DG
S5_ROWS = LT // S5_T
S5_SB = LC // S5_T
S5_NSB = S5_ROWS // S5_SB
S5_RB = 416

ATT_TQ = 2048
ATT_TK = 4096
ATT_KS = 4096
ATT_GQ = 512
ATT_NG = 2 * ATT_TQ // ATT_GQ
ATT_VE = HEAD + 16
LOG2E = math.log2(math.e)

VMEM_LIMIT = 56 * 1024 * 1024


def _params(*sem):
    return pltpu.CompilerParams(dimension_semantics=sem, vmem_limit_bytes=VMEM_LIMIT)


def _const_spec(shape):
    nd = len(shape)
    return pl.BlockSpec(shape, lambda *_: (0,) * nd, pipeline_mode=pl.Buffered(1))


def _bdot(a, b):
    return jnp.dot(a.astype(BF16), b.astype(BF16), preferred_element_type=F32)


def _split(a):
    hi = a.astype(BF16)
    lo = (a - hi.astype(F32)).astype(BF16)
    return hi, lo


def _dot3(a, b):
    ah, al = _split(a)
    bh, bl = _split(b)
    return (jnp.dot(ah, bh, preferred_element_type=F32)
            + jnp.dot(al, bh, preferred_element_type=F32)
            + jnp.dot(ah, bl, preferred_element_type=F32))


def _rms(x, g):
    return x * lax.rsqrt(jnp.mean(x * x, axis=-1, keepdims=True) + EPS) * g


def _mod_kernel(c_ref, w_ref, b_ref, o_ref):
    c = c_ref[...]
    s = c * jax.nn.sigmoid(c)
    o_ref[0] = _dot3(s, w_ref[0]) + b_ref[0]


def _modulation(c, c_ctx, mod_w, mod_b):
    cc = jnp.concatenate([c.reshape(1, D), c_ctx.reshape(1, D), jnp.zeros((6, D), F32)], axis=0)
    tn = 1536
    out = pl.pallas_call(
        _mod_kernel,
        grid=(DEPTH, N_MOD * D // tn),
        in_specs=[pl.BlockSpec((8, D), lambda i, j: (0, 0)),
                  pl.BlockSpec((1, D, tn), lambda i, j: (i, 0, j)),
                  pl.BlockSpec((1, 1, tn), lambda i, j: (i, 0, j))],
        out_specs=pl.BlockSpec((1, 8, tn), lambda i, j: (i, 0, j)),
        out_shape=jax.ShapeDtypeStruct((DEPTH, 8, N_MOD * D), F32),
        compiler_params=_params("parallel", "parallel"),
        name="modulation",
    )(cc, mod_w, mod_b.reshape(DEPTH, 1, N_MOD * D))
    return out[:, :2].reshape(DEPTH, 2, N_MOD, D)


def _qknorm_rope(t, gain, cs, sn, hm):
    hi, lo = _split(t * t)
    ms = jnp.dot(hi, hm, preferred_element_type=F32) + jnp.dot(lo, hm, preferred_element_type=F32)
    tn = t * lax.rsqrt(ms + EPS) * gain
    w = t.shape[1]
    lane = lax.broadcasted_iota(jnp.int32, tn.shape, 1)
    swapped = jnp.where((lane % HEAD) < HEAD // 2, pltpu.roll(tn, w - HEAD // 2, 1), pltpu.roll(tn, HEAD // 2, 1))
    return tn * cs + swapped * sn


def _premix_kernel(x_ref, mod_ref, g_ref, w_ref, cs_ref, sn_ref, qg_ref, kg_ref, hm_ref,
                   a_ref, s_ref, p_ref, q_ref, k_ref, v_ref):
    m = mod_ref[0]
    h = _rms(x_ref[...], g_ref[...]) * (1.0 + m[1:2]) + m[0:1]
    u = jnp.dot(h.astype(BF16), w_ref[...], preferred_element_type=F32)
    a_ref[...] = u[:, 0:768]
    s_ref[...] = u[:, 768:1024]
    p_ref[...] = u[:, 1024:1280]
    cs = cs_ref[...]
    sn = sn_ref[...]
    hm = hm_ref[...]
    q = _qknorm_rope(u[:, 1280:1536], qg_ref[...], jnp.concatenate([cs, cs], axis=1),
                     jnp.concatenate([sn, sn], axis=1), hm) * (ATT_SCALE * LOG2E)
    k = _qknorm_rope(u[:, 1536:1664], kg_ref[...], cs, sn, hm[:128, :128])
    q_ref[...] = q.T.astype(BF16)
    v_ref[...] = u[:, 1664:1792].T.astype(BF16)
    for hh in range(KVH):
        k_ref[hh] = k[:, HEAD * hh:HEAD * (hh + 1)].astype(BF16)


def _premix(x, mods, g, w_in, cs, sn, qg, kg, hm):
    row = lambda w: pl.BlockSpec((TR, w), lambda i: (i, 0))
    return pl.pallas_call(
        _premix_kernel,
        grid=(NRB,),
        in_specs=[row(D),
                  pl.BlockSpec((1, N_MOD, D), lambda i: (i // NLB, 0, 0)),
                  _const_spec((1, D)), _const_spec((D, IN_COLS)),
                  row(128), row(128),
                  _const_spec((1, 256)), _const_spec((1, 128)), _const_spec((256, 256))],
        out_specs=[row(768), row(256), row(256),
                   pl.BlockSpec((QH * HEAD, TR), lambda i: (0, i)),
                   pl.BlockSpec((KVH, TR, HEAD), lambda i: (0, i, 0)),
                   pl.BlockSpec((KVH * HEAD, TR), lambda i: (0, i))],
        out_shape=[jax.ShapeDtypeStruct((LT, 768), F32),
                   jax.ShapeDtypeStruct((LT, 256), F32),
                   jax.ShapeDtypeStruct((LT, 256), F32),
                   jax.ShapeDtypeStruct((QH * HEAD, LT), BF16),
                   jax.ShapeDtypeStruct((KVH, LT, HEAD), BF16),
                   jax.ShapeDtypeStruct((KVH * HEAD, LT), BF16)],
        compiler_params=_params("parallel"),
        name="premix",
    )(x, mods, g, w_in, cs, sn, qg, kg, hm)


def _local_kernel(a_ref, ap_ref, an_ref, p_ref, pp_ref, pn_ref, cw_ref, cb_ref, pw_ref, ps_ref,
                  v_ref, x1_ref, x2_ref, po_ref):
    i = pl.program_id(0)
    is_ctx = i >= NLB
    s0 = jnp.where(is_ctx, L, 0)
    s1 = jnp.where(is_ctx, LT, L)
    n = TR + 2 * HALO
    idx = i * TR - HALO + lax.broadcasted_iota(jnp.int32, (n, 1), 0)
    valid = (idx >= s0) & (idx < s1)

    def shifted(e, d):
        return pltpu.roll(e, (-d) % n, 0)[HALO:HALO + TR] if d else e[HALO:HALO + TR]

    ea = jnp.where(valid, jnp.concatenate([ap_ref[...], a_ref[...], an_ref[...]], axis=0), 0.0)
    w = cw_ref[...]
    y = cb_ref[...] + w[0:1] * shifted(ea, -1) + w[1:2] * shifted(ea, 0) + w[2:3] * shifted(ea, 1)
    v_ref[...] = y[:, 0:256]
    x1_ref[...] = y[:, 256:512]
    x2_ref[...] = y[:, 512:768]

    e = jnp.where(valid, jnp.concatenate([pp_ref[...], p_ref[...], pn_ref[...]], axis=0), 0.0)
    w2 = e + pltpu.roll(e, 1, 0)
    w4 = pltpu.roll(w2, 1, 0) + pltpu.roll(w2, n - 1, 0)
    w8 = pltpu.roll(w4, 2, 0) + pltpu.roll(w4, n - 2, 0)
    w16 = pltpu.roll(w8, 4, 0) + pltpu.roll(w8, n - 4, 0)
    lane = lax.broadcasted_iota(jnp.int32, (TR, 256), 1)
    cut = lambda t: t[HALO:HALO + TR]
    sums = jnp.where(lane < 64, cut(w2), jnp.where(lane < 128, cut(w4), jnp.where(lane < 192, cut(w8), cut(w16))))
    half = jnp.where(lane < 64, 1, jnp.where(lane < 128, 2, jnp.where(lane < 192, 4, 8)))
    tl = i * TR + lax.broadcasted_iota(jnp.int32, (TR, 256), 0) - s0
    cnt = jnp.minimum(tl + half, s1 - s0) - jnp.maximum(tl - half, 0)
    pooled = sums / cnt.astype(F32) - p_ref[...]
    po_ref[...] = _bdot(pooled, pw_ref[...]) * ps_ref[...]


def _local(a, p, conv_w, conv_b, pool_bd, pool_scale):
    r8 = TR // HALO
    main = lambda w: pl.BlockSpec((TR, w), lambda i: (i, 0))
    prev = lambda w: pl.BlockSpec((HALO, w), lambda i: (jnp.maximum(i * r8 - 1, 0), 0))
    nxt = lambda w: pl.BlockSpec((HALO, w), lambda i: (jnp.minimum((i + 1) * r8, LT // HALO - 1), 0))
    o = jax.ShapeDtypeStruct((LT, 256), F32)
    return pl.pallas_call(
        _local_kernel,
        grid=(NRB,),
        in_specs=[main(768), prev(768), nxt(768), main(256), prev(256), nxt(256),
                  _const_spec((3, 768)), _const_spec((1, 768)), _const_spec((256, 256)), _const_spec((1, 256))],
        out_specs=[main(256)] * 4,
        out_shape=[o, o, o, o],
        compiler_params=_params("parallel"),
        name="local_ops",
    )(a, a, a, p, p, p, conv_w, conv_b, pool_bd, pool_scale)


def _filter_block(n0, rows, inv_len, w1, b1, w2, b2, w3f, w3b, dec_f, dec_b):
    half = rows // 2
    m = half + HALO
    idx = lax.broadcasted_iota(jnp.int32, (m, 128), 0)
    lane = lax.broadcasted_iota(jnp.int32, (m, 128), 1)
    ln = lane % HY_FFN
    t = (n0 - HALO + idx + jnp.where(lane < HY_FFN, 0, half)).astype(F32) * inv_len
    fr = jnp.where(ln <= HY_FREQS, ln, ln - HY_FREQS).astype(F32)
    ang = (2.0 * math.pi * t) * fr + jnp.where(ln <= HY_FREQS, 0.5 * math.pi, 0.0)
    feats = jnp.where(ln == 0, t, jnp.where(ln <= 2 * HY_FREQS, jnp.sin(ang), 0.0))
    h = jnp.sin(HY_SIN_FREQ * (_dot3(feats, w1) + b1))
    h = jnp.sin(HY_SIN_FREQ * (_dot3(h, w2) + b2))
    h_cur = h[HALO:]
    h_prev = pltpu.roll(h, 1, 0)[HALO:]
    n = (n0 + lax.broadcasted_iota(jnp.int32, (rows, 1), 0)).astype(F32)
    t_row = n * inv_len
    taps = lambda hh, w3: jnp.concatenate([_dot3(hh, w3[0]), _dot3(hh, w3[1])], axis=0)
    hf = taps(h_cur, w3f) * jnp.exp(-t_row * jnp.abs(dec_f))
    hb = taps(h_prev, w3b) * jnp.exp(-(t_row - inv_len) * jnp.abs(dec_b))
    hb = jnp.where(n >= 1.0, hb, 0.0)
    return hf, hb


def _filter_kernel(w1_ref, b1_ref, w2_ref, b2_ref, w3f_ref, w3b_ref, df_ref, db_ref, o_ref, *, rows):
    hf, hb = _filter_block(pl.program_id(0) * rows, rows, 1.0 / L, w1_ref[...], b1_ref[...], w2_ref[...],
                           b2_ref[...], w3f_ref[...], w3b_ref[...], df_ref[...], db_ref[...])
    o_ref[:, 0:256] = hf[:, 0:256]
    o_ref[:, 256:512] = hb[:, 0:256]
    o_ref[:, 512:768] = hf[:, 256:512]
    o_ref[:, 768:1024] = hb[:, 256:512]


def _filter_gen(fp):
    rows = 512
    return pl.pallas_call(
        functools.partial(_filter_kernel, rows=rows),
        grid=(L // rows,),
        in_specs=[_const_spec((128, 128)), _const_spec((1, 128)), _const_spec((128, 128)),
                  _const_spec((1, 128)), _const_spec((2, 128, 512)), _const_spec((2, 128, 512)),
                  _const_spec((1, 512)), _const_spec((1, 512))],
        out_specs=pl.BlockSpec((rows, 1024), lambda i: (i, 0)),
        out_shape=jax.ShapeDtypeStruct((L, 1024), F32),
        compiler_params=_params("parallel"),
        name="hyena_filter",
    )(*fp)


def _fft_fwd1_kernel(x_ref, m_ref, o_ref):
    x2 = x_ref.reshape(FFT_R * FFT_TJ, 128)
    o2 = o_ref.reshape(2 * FFT_K1 * FFT_TJ, 128)
    for jj in range(FFT_TJ):
        xj = x2[pl.ds(jj, FFT_R, stride=FFT_TJ), :]
        res = jnp.dot(m_ref[...], xj.astype(BF16), preferred_element_type=F32)
        o2[pl.ds(jj, FFT_K1, stride=FFT_TJ), :] = res[:FFT_K1]
        o2[pl.ds(FFT_K1 * FFT_TJ + jj, FFT_K1, stride=FFT_TJ), :] = res[FFT_K1:]


def _fft_fwd1(x3, m1):
    c = x3.shape[2]
    return pl.pallas_call(
        _fft_fwd1_kernel,
        grid=(FFT_N2 // FFT_TJ, c // 128),
        in_specs=[pl.BlockSpec((FFT_R, FFT_TJ, 128), lambda j, cc: (0, j, cc)), _const_spec((FFT_N1, FFT_R))],
        out_specs=pl.BlockSpec((2, FFT_K1, FFT_TJ, 128), lambda j, cc: (0, 0, j, cc)),
        out_shape=jax.ShapeDtypeStruct((2, FFT_K1, FFT_N2, c), F32),
        compiler_params=_params("parallel", "parallel"),
        name="fft_fwd1",
    )(x3, m1)


def _filt_spec_kernel(a_ref, w_ref, o_ref):
    for j in range(FFT_TK):
        a = jnp.concatenate([a_ref[0, j], a_ref[1, j]], axis=0).astype(BF16)
        x = jnp.dot(w_ref[j], a, preferred_element_type=F32)
        o_ref[0, 0, j] = x[:FFT_N2, 0:256] + x[:FFT_N2, 256:512]
        o_ref[0, 1, j] = x[FFT_N2:, 0:256] - x[FFT_N2:, 256:512]


def _filt_spec(fa, w):
    return pl.pallas_call(
        _filt_spec_kernel,
        grid=(HY_ORDER, FFT_K1 // FFT_TK),
        in_specs=[pl.BlockSpec((2, FFT_TK, FFT_N2, 512), lambda o, i: (0, i, 0, o)),
                  pl.BlockSpec((FFT_TK, 256, 256), lambda o, i: (i, 0, 0))],
        out_specs=pl.BlockSpec((1, 2, FFT_TK, FFT_N2, 256), lambda o, i: (o, 0, i, 0, 0)),
        out_shape=jax.ShapeDtypeStruct((HY_ORDER, 2, FFT_K1, FFT_N2, 256), F32),
        compiler_params=_params("parallel", "parallel"),
        name="hyena_filter_spectrum",
    )(fa, w)


def _fft_mid_kernel(a_ref, w_ref, wt_ref, k_ref, o_ref):
    for j in range(FFT_TK):
        a = jnp.concatenate([a_ref[0, j], a_ref[1, j]], axis=0).astype(BF16)
        x = jnp.dot(w_ref[j], a, preferred_element_type=F32)
        xr, xi = x[:FFT_N2], x[FFT_N2:]
        kr, ki = k_ref[0, 0, j], k_ref[0, 1, j]
        y = jnp.concatenate([xr * kr - xi * ki, xr * ki + xi * kr], axis=0).astype(BF16)
        b = jnp.dot(wt_ref[j], y, preferred_element_type=F32)
        o_ref[0, j] = b[:FFT_N2]
        o_ref[1, j] = b[FFT_N2:]


def _fft_mid(a, w, wt, kf, order):
    blk = pl.BlockSpec((2, FFT_TK, FFT_N2, 256), lambda i: (0, i, 0, 0))
    wblk = pl.BlockSpec((FFT_TK, 256, 256), lambda i: (i, 0, 0))
    return pl.pallas_call(
        _fft_mid_kernel,
        grid=(FFT_K1 // FFT_TK,),
        in_specs=[blk, wblk, wblk,
                  pl.BlockSpec((1, 2, FFT_TK, FFT_N2, 256), lambda i: (order, 0, i, 0, 0))],
        out_specs=blk,
        out_shape=jax.ShapeDtypeStruct((2, FFT_K1, FFT_N2, 256), F32),
        compiler_params=_params("parallel"),
        name="fft_mid",
    )(a, w, wt, kf)


def _fft_inv1_kernel(b_ref, m_ref, x_ref, z_ref, fb_ref, o_ref):
    b2 = b_ref.reshape(2 * FFT_K1 * FFT_TJ, 128)
    x2 = x_ref.reshape(FFT_R * FFT_TJ, 128)
    z2 = z_ref.reshape(FFT_R * FFT_TJ, 128)
    o2 = o_ref.reshape(FFT_R * FFT_TJ, 128)
    for jj in range(FFT_TJ):
        b = jnp.concatenate([b2[pl.ds(jj, FFT_K1, stride=FFT_TJ), :],
                             b2[pl.ds(FFT_K1 * FFT_TJ + jj, FFT_K1, stride=FFT_TJ), :]], axis=0).astype(BF16)
        y = jnp.dot(m_ref[...], b, preferred_element_type=F32)
        rows = pl.ds(jj, FFT_R, stride=FFT_TJ)
        o2[rows, :] = x2[rows, :] * (y + fb_ref[...] * z2[rows, :])


def _fft_inv1(b4, minv, gate3, z3, fb):
    sig = pl.BlockSpec((FFT_R, FFT_TJ, 128), lambda j, cc: (0, j, cc))
    return pl.pallas_call(
        _fft_inv1_kernel,
        grid=(FFT_N2 // FFT_TJ, 2),
        in_specs=[pl.BlockSpec((2, FFT_K1, FFT_TJ, 128), lambda j, cc: (0, 0, j, cc)),
                  _const_spec((FFT_R, FFT_N1)), sig, sig, pl.BlockSpec((1, 128), lambda j, cc: (0, cc))],
        out_specs=sig,
        out_shape=jax.ShapeDtypeStruct((FFT_R, FFT_N2, 256), F32),
        compiler_params=_params("parallel", "parallel"),
        name="fft_inv1_gate",
    )(b4, minv, gate3, z3, fb)


def _ctx_hyena_kernel(a_ref, cw_ref, cb_ref, w1_ref, b1_ref, w2_ref, b2_ref, w3f_ref, w3b_ref, df_ref, db_ref,
                      fb_ref, fc_ref, fi_ref, o_ref):
    zero = jnp.zeros((HALO, 768), F32)
    n = LC + 2 * HALO
    ea = jnp.concatenate([zero, a_ref[...], zero], axis=0)
    w = cw_ref[...]
    y = (cb_ref[...] + w[0:1] * pltpu.roll(ea, 1, 0)[HALO:HALO + LC] + w[1:2] * ea[HALO:HALO + LC]
         + w[2:3] * pltpu.roll(ea, n - 1, 0)[HALO:HALO + LC])
    hf, hb = _filter_block(0, LC, 1.0 / LC, w1_ref[...], b1_ref[...], w2_ref[...], b2_ref[...],
                           w3f_ref[...], w3b_ref[...], df_ref[...], db_ref[...])
    fc = fc_ref[...]
    fi = fi_ref[...]
    sf = _dot3(fc, hf)
    sb = _dot3(fc, hb)
    kr = sf[:LC] + sb[:LC]
    ki = sf[LC:] - sb[LC:]
    z = y[:, 0:256]
    fb = fb_ref[...]
    for o in range(HY_ORDER):
        s = _dot3(fc, z)
        sr, si = s[:LC], s[LC:]
        c0 = 256 * o
        yr = sr * kr[:, c0:c0 + 256] - si * ki[:, c0:c0 + 256]
        yi = sr * ki[:, c0:c0 + 256] + si * kr[:, c0:c0 + 256]
        conv = _dot3(fi, jnp.concatenate([yr, yi], axis=0))
        z = y[:, 256 * (o + 1):256 * (o + 2)] * (conv + fb[o:o + 1] * z)
    o_ref[...] = z


def _ctx_hyena(a, conv_w, conv_b, fp, fbias, fc, fi):
    return pl.pallas_call(
        _ctx_hyena_kernel,
        grid=(1,),
        in_specs=[pl.BlockSpec((LC, 768), lambda i: (L // LC, 0)),
                  _const_spec((3, 768)), _const_spec((1, 768)),
                  _const_spec((128, 128)), _const_spec((1, 128)), _const_spec((128, 128)),
                  _const_spec((1, 128)), _const_spec((2, 128, 512)), _const_spec((2, 128, 512)),
                  _const_spec((1, 512)), _const_spec((1, 512)),
                  _const_spec((HY_ORDER, 256)), _const_spec((2 * LC, LC)), _const_spec((LC, 2 * LC))],
        out_specs=pl.BlockSpec((LC, 256), lambda i: (0, 0)),
        out_shape=jax.ShapeDtypeStruct((LC, 256), F32),
        compiler_params=_params("arbitrary"),
        name="hyena_ctx",
    )(a, conv_w, conv_b, *fp, fbias, fc, fi)


def _s5_in_kernel(ua_ref, ub_ref, w_ref, o_ref):
    u = jnp.concatenate([r[pl.ds(j, S5_RB, stride=S5_T), :].astype(BF16)
                         for j in range(S5_T) for r in (ua_ref, ub_ref)], axis=1)
    o_ref[0] = jnp.dot(u, w_ref[0], preferred_element_type=F32)


def _s5_in(u2, w3):
    return pl.pallas_call(
        _s5_in_kernel,
        grid=(3, S5_ROWS // S5_RB),
        in_specs=[pl.BlockSpec((S5_RB * S5_T, DG // 2), lambda m, r: (r, 0)),
                  pl.BlockSpec((S5_RB * S5_T, DG // 2), lambda m, r: (r, 1)),
                  pl.BlockSpec((1, S5_W, S5_W), lambda m, r: (m, 0, 0))],
        out_specs=pl.BlockSpec((1, S5_RB, S5_W), lambda m, r: (m, r, 0)),
        out_shape=jax.ShapeDtypeStruct((3, S5_ROWS, S5_W), F32),
        compiler_params=_params("parallel", "parallel"),
        name="s5_chunk_in",
    )(u2, u2, w3)


def _s5_scan_kernel(f_ref, b_ref, lam_ref, hf_ref, hb_ref, sf, sb, bf, bb):
    j = pl.program_id(0)

    @pl.when(j == 0)
    def _():
        sf[...] = jnp.zeros_like(sf)
        sb[...] = jnp.zeros_like(sb)

    half = S5_W // 2
    lam = lam_ref[...]
    lfr, lfi = lam[0:1, :half], lam[0:1, half:]
    lbr, lbi = lam[1:2, :half], lam[1:2, half:]

    def cmul(lr, li, h):
        hr, hi = h[:, :half], h[:, half:]
        return jnp.concatenate([lr * hr - li * hi, lr * hi + li * hr], axis=1)

    def body(r, carry):
        hf, hb = carry
        bf[pl.ds(r, 1), :] = hf
        hf = cmul(lfr, lfi, hf) + f_ref[0, pl.ds(r, 1), :]
        rb = S5_SB - 1 - r
        bb[pl.ds(rb, 1), :] = hb
        hb = cmul(lbr, lbi, hb) + b_ref[0, pl.ds(rb, 1), :]
        return hf, hb

    hf, hb = lax.fori_loop(0, S5_SB, body, (sf[...], sb[...]))
    sf[...] = hf
    sb[...] = hb
    hf_ref[...] = bf[...].astype(BF16)
    hb_ref[...] = bb[...].astype(BF16)


def _s5_scan(fb, lam8):
    fidx = lambda j: jnp.where(j == 0, S5_NSB - 1, j - 1)
    bidx = lambda j: S5_NSB - 1 - j
    o = jax.ShapeDtypeStruct((S5_ROWS, S5_W), BF16)
    return pl.pallas_call(
        _s5_scan_kernel,
        grid=(S5_NSB,),
        in_specs=[pl.BlockSpec((1, S5_SB, S5_W), lambda j: (1, fidx(j), 0)),
                  pl.BlockSpec((1, S5_SB, S5_W), lambda j: (2, bidx(j), 0)),
                  _const_spec((2, S5_W))],
        out_specs=[pl.BlockSpec((S5_SB, S5_W), lambda j: (fidx(j), 0)),
                   pl.BlockSpec((S5_SB, S5_W), lambda j: (bidx(j), 0))],
        out_shape=[o, o],
        scratch_shapes=[pltpu.VMEM((1, S5_W), F32), pltpu.VMEM((1, S5_W), F32),
                        pltpu.VMEM((S5_SB, S5_W), F32), pltpu.VMEM((S5_SB, S5_W), F32)],
        compiler_params=_params("arbitrary"),
        name="s5_scan",
    )(fb, fb, lam8)


def _s5_out_kernel(y_ref, hf_ref, hb_ref, wf_ref, wb_ref, o_ref, y_sc):
    half = pl.program_id(1)

    @pl.when(half == 0)
    def _():
        y_sc[...] = (y_ref[0] + jnp.dot(hf_ref[...], wf_ref[...], preferred_element_type=F32)
                     + jnp.dot(hb_ref[...], wb_ref[...], preferred_element_type=F32))

    for hh in range(2):
        @pl.when(half == hh)
        def _():
            for j in range(S5_T):
                c0 = DG * j + 128 * hh
                o_ref[pl.ds(j, S5_RB, stride=S5_T), :] = y_sc[:, c0:c0 + 128]


def _s5_out(fb, hf, hb, wf, wb):
    rows = pl.BlockSpec((S5_RB, S5_W), lambda r, c: (r, 0))
    return pl.pallas_call(
        _s5_out_kernel,
        grid=(S5_ROWS // S5_RB, 2),
        in_specs=[pl.BlockSpec((1, S5_RB, S5_W), lambda r, c: (0, r, 0)), rows, rows,
                  _const_spec((S5_W, S5_W)), _const_spec((S5_W, S5_W))],
        out_specs=pl.BlockSpec((S5_RB * S5_T, 128), lambda r, c: (r, c)),
        out_shape=jax.ShapeDtypeStruct((LT, DG), F32),
        scratch_shapes=[pltpu.VMEM((S5_RB, S5_W), F32)],
        compiler_params=_params("parallel", "arbitrary"),
        name="s5_chunk_out",
    )(fb, hf, hb, wf, wb)


def _value_rows(vt):
    return jnp.concatenate([vt, jnp.ones((ATT_VE - HEAD, vt.shape[1]), BF16)], axis=0)


def _attend(k, ve, q_ref, m_sc, acc_sc):
    ks = min(ATT_KS, k.shape[0])
    units = [(c, g) for c in range(k.shape[0] // ks) for g in range(ATT_NG)]

    def scores(unit):
        c, g = unit
        j, c0 = divmod(g * ATT_GQ, ATT_TQ)
        return jnp.dot(k[c * ks:(c + 1) * ks], q_ref[HEAD * j:HEAD * (j + 1), c0:c0 + ATT_GQ],
                       preferred_element_type=F32)

    s_next = scores(units[0])
    for u, (c, g) in enumerate(units):
        s = s_next
        if u + 1 < len(units):
            s_next = scores(units[u + 1])
        m_prev = m_sc[g]
        m_new = jnp.maximum(m_prev, jnp.max(s, axis=0, keepdims=True))
        alpha = jnp.exp2(m_prev - m_new)
        p = jnp.exp2(s - m_new).astype(BF16)
        acc_sc[g] = alpha * acc_sc[g] + jnp.dot(ve[:, c * ks:(c + 1) * ks], p, preferred_element_type=F32)
        m_sc[g] = m_new


def _flash_kernel(q_ref, k_ref, v_ref, kc_ref, vc_ref, o_ref, m_sc, acc_sc):
    ki = pl.program_id(2)

    @pl.when(ki == 0)
    def _():
        m_sc[...] = jnp.full_like(m_sc, -1e30)
        acc_sc[...] = jnp.zeros_like(acc_sc)

    _attend(k_ref[0], _value_rows(v_ref[...]), q_ref, m_sc, acc_sc)

    @pl.when(ki == pl.num_programs(2) - 1)
    def _():
        _attend(kc_ref[0], _value_rows(vc_ref[...]), q_ref, m_sc, acc_sc)
        for g in range(ATT_NG):
            j, c0 = divmod(g * ATT_GQ, ATT_TQ)
            acc = acc_sc[g]
            o_ref[HEAD * j:HEAD * (j + 1), c0:c0 + ATT_GQ] = (acc[:HEAD] / acc[HEAD:HEAD + 1]).astype(BF16)


def _flash(qt, k, vt):
    return pl.pallas_call(
        _flash_kernel,
        grid=(KVH, L // ATT_TQ, L // ATT_TK),
        in_specs=[pl.BlockSpec((2 * HEAD, ATT_TQ), lambda h, qi, ki: (h, qi)),
                  pl.BlockSpec((1, ATT_TK, HEAD), lambda h, qi, ki: (h, ki, 0)),
                  pl.BlockSpec((HEAD, ATT_TK), lambda h, qi, ki: (h, ki)),
                  pl.BlockSpec((1, LC, HEAD), lambda h, qi, ki: (h, L // LC, 0)),
                  pl.BlockSpec((HEAD, LC), lambda h, qi, ki: (h, L // LC))],
        out_specs=pl.BlockSpec((2 * HEAD, ATT_TQ), lambda h, qi, ki: (h, qi)),
        out_shape=jax.ShapeDtypeStruct((QH * HEAD, L), BF16),
        scratch_shapes=[pltpu.VMEM((ATT_NG, 1, ATT_GQ), F32), pltpu.VMEM((ATT_NG, ATT_VE, ATT_GQ), F32)],
        compiler_params=_params("parallel", "parallel", "arbitrary"),
        name="flash_attention",
    )(qt, k, vt, k, vt)


def _ctx_attn_kernel(q_ref, k_ref, v_ref, o_ref):
    ve = _value_rows(v_ref[...])
    for j in range(2):
        s = jnp.dot(k_ref[0], q_ref[HEAD * j:HEAD * (j + 1), :], preferred_element_type=F32)
        p = jnp.exp2(s - jnp.max(s, axis=0, keepdims=True)).astype(BF16)
        acc = jnp.dot(ve, p, preferred_element_type=F32)
        o_ref[HEAD * j:HEAD * (j + 1), :] = (acc[:HEAD] / acc[HEAD:HEAD + 1]).astype(BF16)


def _ctx_attn(qt, k, vt):
    return pl.pallas_call(
        _ctx_attn_kernel,
        grid=(KVH,),
        in_specs=[pl.BlockSpec((2 * HEAD, LC), lambda h: (h, L // LC)),
                  pl.BlockSpec((1, LC, HEAD), lambda h: (h, L // LC, 0)),
                  pl.BlockSpec((HEAD, LC), lambda h: (h, L // LC))],
        out_specs=pl.BlockSpec((2 * HEAD, LC), lambda h: (h, 0)),
        out_shape=jax.ShapeDtypeStruct((QH * HEAD, LC), BF16),
        compiler_params=_params("parallel"),
        name="ctx_attention",
    )(qt, k, vt)


def _gelu_tanh(x):
    return 0.5 * x * (1.0 + jnp.tanh(math.sqrt(2.0 / math.pi) * (x + 0.044715 * (x * x * x))))


def _post_kernel(x_ref, mod_ref, hyl_ref, hyc_ref, s5_ref, po_ref, atl_ref, atc_ref,
                 gw_ref, gb_ref, wo_ref, g1_ref, g2_ref, g3_ref, w1_ref, w2_ref, o_ref):
    is_ctx = pl.program_id(0) >= NLB
    m = mod_ref[0]
    g = _gelu_tanh(s5_ref[...])
    s5o = g * jax.nn.sigmoid(_bdot(g, gw_ref[...]) + gb_ref[...])
    hy = jnp.where(is_ctx, hyc_ref[...], hyl_ref[...])
    at = jnp.where(is_ctx, atc_ref[...], atl_ref[...]).astype(F32).T
    o = (_bdot(hy, wo_ref[0:256, :]) + _bdot(s5o, wo_ref[256:512, :]) + _bdot(po_ref[...], wo_ref[512:768, :])
         + _bdot(at, wo_ref[768:1024, :]))
    x = x_ref[...] + m[2:3] * _rms(o, g1_ref[...])
    h = _rms(x, g2_ref[...]) * (1.0 + m[4:5]) + m[3:4]
    f = jnp.dot(h.astype(BF16), w1_ref[...], preferred_element_type=F32)
    f = jnp.square(jnp.maximum(f, 0.0)).astype(BF16)
    f = jnp.dot(f, w2_ref[...], preferred_element_type=F32)
    o_ref[...] = x + m[5:6] * _rms(f, g3_ref[...])


def _post(x, mods, hy_l, hy_c, s5y, po, at_l, at_c, glu_w, glu_b, w_out, g1, g2, g3, w1, w2, n_blocks):
    row = lambda w: pl.BlockSpec((TR, w), lambda i: (i, 0))
    lat = lambda i: jnp.minimum(i, NLB - 1)
    return pl.pallas_call(
        _post_kernel,
        grid=(n_blocks,),
        in_specs=[row(D),
                  pl.BlockSpec((1, N_MOD, D), lambda i: (i // NLB, 0, 0)),
                  pl.BlockSpec((TR, 256), lambda i: (lat(i), 0)),
                  _const_spec((LC, 256)),
                  row(256), row(256),
                  pl.BlockSpec((QH * HEAD, TR), lambda i: (0, lat(i))),
                  _const_spec((QH * HEAD, LC)),
                  _const_spec((256, 256)), _const_spec((1, 256)), _const_spec((D, D)),
                  _const_spec((1, D)), _const_spec((1, D)), _const_spec((1, D)),
                  _const_spec((D, D_FF)), _const_spec((D_FF, D))],
        out_specs=row(D),
        out_shape=jax.ShapeDtypeStruct((n_blocks * TR, D), F32),
        compiler_params=_params("parallel"),
        name="post_mix_mlp",
    )(x, mods, hy_l, hy_c, s5y, po, at_l, at_c, glu_w, glu_b, w_out, g1, g2, g3, w1, w2)


def _dft_tables():
    n = jnp.arange(128, dtype=jnp.int32)
    m = ((2 * n[:, None] + 1) * n[None, :]) % (2 * FFT_N1)
    ang = m.astype(F32) * (2.0 * math.pi / (2 * FFT_N1))
    c1, s1 = jnp.cos(ang), jnp.sin(ang)
    m1 = jnp.concatenate([c1, -s1], axis=0)
    m1inv = jnp.concatenate([c1.T, -s1.T], axis=1) * (2.0 / FFT_N)
    kk = n[:, None, None] + FFT_N1 * n[None, :, None]
    mm = ((2 * kk + 1) * n[None, None, :]) % (2 * FFT_N)
    phi = mm.astype(F32) * (2.0 * math.pi / (2 * FFT_N))
    cm, sm = jnp.cos(phi), jnp.sin(phi)
    w = jnp.concatenate([jnp.concatenate([cm, sm], axis=2), jnp.concatenate([-sm, cm], axis=2)], axis=1)
    wt = jnp.swapaxes(w, 1, 2)
    nc = jnp.arange(LC, dtype=jnp.int32)
    mc = ((2 * nc[:, None] + 1) * nc[None, :]) % (4 * LC)
    angc = mc.astype(F32) * (2.0 * math.pi / (4 * LC))
    cc, sc = jnp.cos(angc), jnp.sin(angc)
    fc = jnp.concatenate([cc, -sc], axis=0)
    fi = jnp.concatenate([cc.T, -sc.T], axis=1) * (2.0 / (2 * LC))
    return m1.astype(BF16), m1inv.astype(BF16), w.astype(BF16), wt.astype(BF16), fc, fi


def _rope_tables():
    t = jnp.arange(L, dtype=jnp.int32)
    inv = ROPE_THETA ** (-jnp.arange(0, HEAD // 2, 2, dtype=F32) / (HEAD // 2))
    ang = jnp.concatenate([(t // GRID_W).astype(F32)[:, None] * inv[None, :],
                           (t % GRID_W).astype(F32)[:, None] * inv[None, :]], axis=-1)
    ang = jnp.concatenate([ang, jnp.zeros((LC, HEAD // 2), F32)], axis=0)
    c, s = jnp.cos(ang), jnp.sin(ang)
    return jnp.concatenate([c, c, c, c], axis=1), jnp.concatenate([-s, s, -s, s], axis=1)


def _s5_tables(a_re, a_im, log_dt, b_re, b_im, c_re, c_im, d):
    dt = jnp.exp(log_dt)[..., None]
    tau = jnp.arange(S5_T + 1, dtype=F32)[:, None, None, None]
    mag = jnp.exp(a_re * dt * tau)
    pr, pi = mag * jnp.cos(a_im * dt * tau), mag * jnp.sin(a_im * dt * tau)
    lam_re, lam_im = pr[1], pi[1]
    den = a_re * a_re + a_im * a_im
    nr, ni = lam_re - 1.0, lam_im
    cr = (nr * a_re + ni * a_im) / den
    ci = (ni * a_re - nr * a_im) / den
    bb_re = cr[..., None] * b_re - ci[..., None] * b_im
    bb_im = cr[..., None] * b_im + ci[..., None] * b_re
    cl_re = c_re[None] * pr[:, :, :, None, :] - c_im[None] * pi[:, :, :, None, :]
    cl_im = c_re[None] * pi[:, :, :, None, :] + c_im[None] * pr[:, :, :, None, :]
    kk = jnp.einsum('tdghp,dgpk->tdghk', cl_re, bb_re) - jnp.einsum('tdghp,dgpk->tdghk', cl_im, bb_im)
    k0 = kk[0, 0] + kk[0, 1] + d.reshape(S5_G, S5_H)[:, :, None] * jnp.eye(S5_H, dtype=F32)[None]
    kfull = jnp.concatenate([kk[1:S5_T, 1][::-1], k0[None], kk[1:S5_T, 0]], axis=0)
    st = jnp.arange(S5_T)
    kt = kfull[st[None, :] - st[:, None] + S5_T - 1]
    th = S5_T * S5_H
    expand = jnp.einsum('tu,hk->thuk', jnp.eye(S5_T, dtype=F32), jnp.eye(S5_H, dtype=F32))
    expand = jnp.broadcast_to(expand[:, :, :, None, :], (S5_T, S5_H, S5_T, S5_G, S5_H)).reshape(th, S5_W)
    col_g = (jnp.arange(S5_W) // S5_H) % S5_G
    row_g_u = (jnp.arange(S5_W) // S5_H) % S5_G
    row_g_st = (jnp.arange(S5_W) // S5_P) % S5_G
    st_col_g = jnp.arange(S5_G * S5_P) // S5_P

    def widen(compact, row_g):
        full = jnp.dot(compact.astype(BF16), expand.astype(BF16), preferred_element_type=F32)
        return jnp.where(row_g[:, None] == col_g[None, :], full, 0.0).astype(BF16)

    m_intra = widen(kt.transpose(0, 2, 4, 1, 3).reshape(S5_W, th), row_g_u)

    def state_in(pw_re, pw_im, d_):
        re = pw_re[:, :, :, None] * bb_re[d_][None] - pw_im[:, :, :, None] * bb_im[d_][None]
        im = pw_re[:, :, :, None] * bb_im[d_][None] + pw_im[:, :, :, None] * bb_re[d_][None]

        def wide(x):
            c = x.transpose(0, 1, 3, 2).reshape(S5_W, S5_P)
            return jnp.where(row_g_u[:, None] == st_col_g[None, :], jnp.tile(c, (1, S5_G)), 0.0)
        return jnp.concatenate([wide(re), wide(im)], axis=1).astype(BF16)

    m_fst = state_in(pr[:S5_T, 0][::-1], pi[:S5_T, 0][::-1], 0)
    m_bst = state_in(pr[:S5_T, 1], pi[:S5_T, 1], 1)

    def state_out(cre, cim):
        both = jnp.stack([cre, -cim], axis=0)
        return widen(both.transpose(0, 2, 4, 1, 3).reshape(S5_W, th), row_g_st)

    m_fout = state_out(cl_re[1:, 0], cl_im[1:, 0])
    m_bout = state_out(cl_re[1:, 1][::-1], cl_im[1:, 1][::-1])
    lam8 = jnp.concatenate([pr[S5_T].reshape(2, -1), pi[S5_T].reshape(2, -1)], axis=1)
    return jnp.stack([m_intra, m_fst, m_bst], axis=0), lam8, m_fout, m_bout


def _filter_params(w1, b1, w2, b2, w3, decay):
    zf = jnp.zeros((HY_FFN, HY_FFN), F32)
    w1p = jnp.concatenate([w1, jnp.zeros((HY_FFN - w1.shape[0], HY_FFN), F32)], axis=0)
    pair = lambda w: jnp.concatenate([jnp.concatenate([w, zf], axis=1), jnp.concatenate([zf, w], axis=1)], axis=0)
    two = lambda b: jnp.concatenate([b, b]).reshape(1, 2 * HY_FFN)
    w3r = w3.reshape(HY_FFN, HY_ORDER, 2, DG)
    z3 = jnp.zeros((HY_FFN, 512), F32)
    halves = lambda w: jnp.stack([jnp.concatenate([w, z3], axis=0), jnp.concatenate([z3, w], axis=0)], axis=0)
    return (pair(w1p), two(b1), pair(w2), two(b2),
            halves(w3r[:, :, 0].reshape(HY_FFN, 512)), halves(w3r[:, :, 1].reshape(HY_FFN, 512)),
            decay[:, 0].reshape(1, 512), decay[:, 1].reshape(1, 512))


def _block_diag(w):
    g, n, _ = w.shape
    return jnp.einsum('gcd,gj->gcjd', w, jnp.eye(g, dtype=w.dtype)).reshape(g * n, g * n)


def kernel(x, c, ctx, c_ctx, mod_w, mod_b, norm_pre_mix, norm_post_mix, norm_pre_mlp, norm_post_mlp,
           w_in, w_out, hy_conv_w, hy_conv_b, hy_ffn_w1, hy_ffn_b1, hy_ffn_w2, hy_ffn_b2, hy_ffn_w3,
           hy_decay, hy_bias, s5_a_re, s5_a_im, s5_log_dt, s5_b_re, s5_b_im, s5_c_re, s5_c_im, s5_d,
           s5_glu_w, s5_glu_b, pool_w, pool_scale, att_q_norm, att_k_norm, mlp_w1, mlp_w2):
    xs = jnp.concatenate([x[0], ctx[0]], axis=0)
    mods = _modulation(c, c_ctx, mod_w, mod_b)
    m1, m1inv, wk, wkt, fc, fi = _dft_tables()
    cs, sn = _rope_tables()
    perm = jnp.concatenate([jnp.arange(0, HEAD, 2), jnp.arange(1, HEAD, 2)])
    qcols = 1280 + (jnp.arange(QH)[:, None] * HEAD + perm[None, :]).reshape(-1)
    kcols = 1536 + (jnp.arange(KVH)[:, None] * HEAD + perm[None, :]).reshape(-1)
    cols = jnp.concatenate([jnp.arange(1280), qcols, kcols, jnp.arange(1664, IN_COLS)])
    head_mean = _block_diag(jnp.full((QH, HEAD, HEAD), 1.0 / HEAD, F32)).astype(BF16)

    for i in range(DEPTH):
        w_in_i = w_in[i][:, cols].astype(BF16)
        qg = jnp.tile(att_q_norm[i][perm], QH).reshape(1, 256)
        kg = jnp.tile(att_k_norm[i][perm], KVH).reshape(1, 128)
        a, s, p, q, k, v = _premix(xs, mods[i], norm_pre_mix[i].reshape(1, D), w_in_i, cs, sn, qg, kg, head_mean)

        vv, x1, x2, po = _local(a, p, hy_conv_w[i], hy_conv_b[i].reshape(1, 768),
                                _block_diag(pool_w[i]), pool_scale[i].reshape(1, 256))

        fp = _filter_params(hy_ffn_w1[i], hy_ffn_b1[i], hy_ffn_w2[i], hy_ffn_b2[i], hy_ffn_w3[i], hy_decay[i])
        filt = _filter_gen(fp)
        fa = _fft_fwd1(filt.reshape(FFT_R, FFT_N2, 1024), m1)
        kf = _filt_spec(fa, wk)
        time_major = lambda t: t.reshape(LT // FFT_N2, FFT_N2, 256)
        z = time_major(vv)
        gates = (time_major(x1), time_major(x2))
        for o in range(HY_ORDER):
            fa_z = _fft_fwd1(z, m1)
            bm = _fft_mid(fa_z, wk, wkt, kf, o)
            z = _fft_inv1(bm, m1inv, gates[o], z, hy_bias[i][o].reshape(1, 256))
        hy_l = z.reshape(L, 256)
        hy_c = _ctx_hyena(a, hy_conv_w[i], hy_conv_b[i].reshape(1, 768), fp, hy_bias[i], fc, fi)

        w3, lam8, m_fout, m_bout = _s5_tables(s5_a_re[i], s5_a_im[i], s5_log_dt[i], s5_b_re[i], s5_b_im[i],
                                              s5_c_re[i], s5_c_im[i], s5_d[i])
        fb = _s5_in(s, w3)
        hf, hb = _s5_scan(fb, lam8)
        s5y = _s5_out(fb, hf, hb, m_fout, m_bout)

        at_l = _flash(q, k, v)
        at_c = _ctx_attn(q, k, v)

        xs = _post(xs, mods[i], hy_l, hy_c, s5y, po, at_l, at_c,
                   s5_glu_w[i].astype(BF16), s5_glu_b[i].reshape(1, 256), w_out[i].astype(BF16),
                   norm_post_mix[i].reshape(1, D), norm_pre_mlp[i].reshape(1, D), norm_post_mlp[i].reshape(1, D),
                   mlp_w1[i].astype(BF16), mlp_w2[i].astype(BF16),
                   NRB if i < DEPTH - 1 else NLB)
    return xs[None]
```

```python
import functools
import math

import jax
import jax.numpy as jnp
from jax import lax
from jax.experimental import pallas as pl
from jax.experimental.pallas import tpu as pltpu

F32 = jnp.float32
BF16 = jnp.bfloat16

D = 1024
L = 16384
LC = 256
LT = L + LC
DEPTH = 4
GRID_W = 64
EPS = 1e-6
N_MOD = 6
DG = 256
HY_ORDER = 2
HY_FREQS = 16
HY_FFN = 64
HY_SIN_FREQ = 1.0
S5_H = 16
S5_G = 16
S5_P = 64
HEAD = 64
QH = 4
KVH = 2
ATT_SCALE = 1.0 / math.sqrt(HEAD)
ROPE_THETA = 10000.0
D_FF = 4 * D
IN_COLS = 1792

TR = 256
NLB = L // TR
NRB = LT // TR
HALO = 8

FFT_N = 2 * L
FFT_N1 = 256
FFT_N2 = 128
FFT_K1 = FFT_N1 // 2
FFT_R = L // FFT_N2
FFT_TJ = 16
FFT_TK = 8

S5_T = 8
S5_W = S5_T * DG
S5_ROWS = LT // S5_T
S5_SB = LC // S5_T
S5_NSB = S5_ROWS // S5_SB
S5_RB = 416

ATT_TQ = 4096
ATT_TK = 4096
ATT_KS = 4096
ATT_GQ = 512
ATT_NG = 2 * ATT_TQ // ATT_GQ
ATT_VE = HEAD + 16
LOG2E = math.log2(math.e)

VMEM_LIMIT = 56 * 1024 * 1024


def _params(*sem):
    return pltpu.CompilerParams(dimension_semantics=sem, vmem_limit_bytes=VMEM_LIMIT)


def _const_spec(shape):
    nd = len(shape)
    return pl.BlockSpec(shape, lambda *_: (0,) * nd, pipeline_mode=pl.Buffered(1))


def _bdot(a, b):
    return jnp.dot(a.astype(BF16), b.astype(BF16), preferred_element_type=F32)


def _split(a):
    hi = a.astype(BF16)
    lo = (a - hi.astype(F32)).astype(BF16)
    return hi, lo


def _dot3(a, b):
    ah, al = _split(a)
    bh, bl = _split(b)
    return (jnp.dot(ah, bh, preferred_element_type=F32)
            + jnp.dot(al, bh, preferred_element_type=F32)
            + jnp.dot(ah, bl, preferred_element_type=F32))


def _rms(x, g):
    return x * lax.rsqrt(jnp.mean(x * x, axis=-1, keepdims=True) + EPS) * g


def _mod_kernel(c_ref, w_ref, b_ref, o_ref):
    c = c_ref[...]
    s = c * jax.nn.sigmoid(c)
    o_ref[0] = _dot3(s, w_ref[0]) + b_ref[0]


def _modulation(c, c_ctx, mod_w, mod_b):
    cc = jnp.concatenate([c.reshape(1, D), c_ctx.reshape(1, D), jnp.zeros((6, D), F32)], axis=0)
    tn = 1536
    out = pl.pallas_call(
        _mod_kernel,
        grid=(DEPTH, N_MOD * D // tn),
        in_specs=[pl.BlockSpec((8, D), lambda i, j: (0, 0)),
                  pl.BlockSpec((1, D, tn), lambda i, j: (i, 0, j)),
                  pl.BlockSpec((1, 1, tn), lambda i, j: (i, 0, j))],
        out_specs=pl.BlockSpec((1, 8, tn), lambda i, j: (i, 0, j)),
        out_shape=jax.ShapeDtypeStruct((DEPTH, 8, N_MOD * D), F32),
        compiler_params=_params("parallel", "parallel"),
        name="modulation",
    )(cc, mod_w, mod_b.reshape(DEPTH, 1, N_MOD * D))
    return out[:, :2].reshape(DEPTH, 2, N_MOD, D)


def _qknorm_rope(t, gain, cs, sn, hm):
    hi, lo = _split(t * t)
    ms = jnp.dot(hi, hm, preferred_element_type=F32) + jnp.dot(lo, hm, preferred_element_type=F32)
    tn = t * lax.rsqrt(ms + EPS) * gain
    w = t.shape[1]
    lane = lax.broadcasted_iota(jnp.int32, tn.shape, 1)
    swapped = jnp.where((lane % HEAD) < HEAD // 2, pltpu.roll(tn, w - HEAD // 2, 1), pltpu.roll(tn, HEAD // 2, 1))
    return tn * cs + swapped * sn


def _premix_kernel(x_ref, mod_ref, g_ref, w_ref, cs_ref, sn_ref, qg_ref, kg_ref, hm_ref,
                   a_ref, s_ref, p_ref, q_ref, k_ref, v_ref):
    m = mod_ref[0]
    h = _rms(x_ref[...], g_ref[...]) * (1.0 + m[1:2]) + m[0:1]
    u = jnp.dot(h.astype(BF16), w_ref[...], preferred_element_type=F32)
    a_ref[...] = u[:, 0:768]
    s_ref[...] = u[:, 768:1024]
    p_ref[...] = u[:, 1024:1280]
    cs = cs_ref[...]
    sn = sn_ref[...]
    hm = hm_ref[...]
    q = _qknorm_rope(u[:, 1280:1536], qg_ref[...], jnp.concatenate([cs, cs], axis=1),
                     jnp.concatenate([sn, sn], axis=1), hm) * (ATT_SCALE * LOG2E)
    k = _qknorm_rope(u[:, 1536:1664], kg_ref[...], cs, sn, hm[:128, :128])
    q_ref[...] = q.T.astype(BF16)
    v_ref[...] = u[:, 1664:1792].T.astype(BF16)
    for hh in range(KVH):
        k_ref[hh] = k[:, HEAD * hh:HEAD * (hh + 1)].astype(BF16)


def _premix(x, mods, g, w_in, cs, sn, qg, kg, hm):
    row = lambda w: pl.BlockSpec((TR, w), lambda i: (i, 0))
    return pl.pallas_call(
        _premix_kernel,
        grid=(NRB,),
        in_specs=[row(D),
                  pl.BlockSpec((1, N_MOD, D), lambda i: (i // NLB, 0, 0)),
                  _const_spec((1, D)), _const_spec((D, IN_COLS)),
                  row(128), row(128),
                  _const_spec((1, 256)), _const_spec((1, 128)), _const_spec((256, 256))],
        out_specs=[row(768), row(256), row(256),
                   pl.BlockSpec((QH * HEAD, TR), lambda i: (0, i)),
                   pl.BlockSpec((KVH, TR, HEAD), lambda i: (0, i, 0)),
                   pl.BlockSpec((KVH * HEAD, TR), lambda i: (0, i))],
        out_shape=[jax.ShapeDtypeStruct((LT, 768), F32),
                   jax.ShapeDtypeStruct((LT, 256), F32),
                   jax.ShapeDtypeStruct((LT, 256), F32),
                   jax.ShapeDtypeStruct((QH * HEAD, LT), BF16),
                   jax.ShapeDtypeStruct((KVH, LT, HEAD), BF16),
                   jax.ShapeDtypeStruct((KVH * HEAD, LT), BF16)],
        compiler_params=_params("parallel"),
        name="premix",
    )(x, mods, g, w_in, cs, sn, qg, kg, hm)


def _local_kernel(a_ref, ap_ref, an_ref, p_ref, pp_ref, pn_ref, cw_ref, cb_ref, pw_ref, ps_ref,
                  v_ref, x1_ref, x2_ref, po_ref):
    i = pl.program_id(0)
    is_ctx = i >= NLB
    s0 = jnp.where(is_ctx, L, 0)
    s1 = jnp.where(is_ctx, LT, L)
    n = TR + 2 * HALO
    idx = i * TR - HALO + lax.broadcasted_iota(jnp.int32, (n, 1), 0)
    valid = (idx >= s0) & (idx < s1)

    def shifted(e, d):
        return pltpu.roll(e, (-d) % n, 0)[HALO:HALO + TR] if d else e[HALO:HALO + TR]

    ea = jnp.where(valid, jnp.concatenate([ap_ref[...], a_ref[...], an_ref[...]], axis=0), 0.0)
    w = cw_ref[...]
    y = cb_ref[...] + w[0:1] * shifted(ea, -1) + w[1:2] * shifted(ea, 0) + w[2:3] * shifted(ea, 1)
    v_ref[...] = y[:, 0:256]
    x1_ref[...] = y[:, 256:512]
    x2_ref[...] = y[:, 512:768]

    e = jnp.where(valid, jnp.concatenate([pp_ref[...], p_ref[...], pn_ref[...]], axis=0), 0.0)
    w2 = e + pltpu.roll(e, 1, 0)
    w4 = pltpu.roll(w2, 1, 0) + pltpu.roll(w2, n - 1, 0)
    w8 = pltpu.roll(w4, 2, 0) + pltpu.roll(w4, n - 2, 0)
    w16 = pltpu.roll(w8, 4, 0) + pltpu.roll(w8, n - 4, 0)
    lane = lax.broadcasted_iota(jnp.int32, (TR, 256), 1)
    cut = lambda t: t[HALO:HALO + TR]
    sums = jnp.where(lane < 64, cut(w2), jnp.where(lane < 128, cut(w4), jnp.where(lane < 192, cut(w8), cut(w16))))
    half = jnp.where(lane < 64, 1, jnp.where(lane < 128, 2, jnp.where(lane < 192, 4, 8)))
    tl = i * TR + lax.broadcasted_iota(jnp.int32, (TR, 256), 0) - s0
    cnt = jnp.minimum(tl + half, s1 - s0) - jnp.maximum(tl - half, 0)
    pooled = sums / cnt.astype(F32) - p_ref[...]
    po_ref[...] = _bdot(pooled, pw_ref[...]) * ps_ref[...]


def _local(a, p, conv_w, conv_b, pool_bd, pool_scale):
    r8 = TR // HALO
    main = lambda w: pl.BlockSpec((TR, w), lambda i: (i, 0))
    prev = lambda w: pl.BlockSpec((HALO, w), lambda i: (jnp.maximum(i * r8 - 1, 0), 0))
    nxt = lambda w: pl.BlockSpec((HALO, w), lambda i: (jnp.minimum((i + 1) * r8, LT // HALO - 1), 0))
    o = jax.ShapeDtypeStruct((LT, 256), F32)
    return pl.pallas_call(
        _local_kernel,
        grid=(NRB,),
        in_specs=[main(768), prev(768), nxt(768), main(256), prev(256), nxt(256),
                  _const_spec((3, 768)), _const_spec((1, 768)), _const_spec((256, 256)), _const_spec((1, 256))],
        out_specs=[main(256)] * 4,
        out_shape=[o, o, o, o],
        compiler_params=_params("parallel"),
        name="local_ops",
    )(a, a, a, p, p, p, conv_w, conv_b, pool_bd, pool_scale)


def _filter_block(n0, rows, inv_len, w1, b1, w2, b2, w3f, w3b, dec_f, dec_b):
    half = rows // 2
    m = half + HALO
    idx = lax.broadcasted_iota(jnp.int32, (m, 128), 0)
    lane = lax.broadcasted_iota(jnp.int32, (m, 128), 1)
    ln = lane % HY_FFN
    t = (n0 - HALO + idx + jnp.where(lane < HY_FFN, 0, half)).astype(F32) * inv_len
    fr = jnp.where(ln <= HY_FREQS, ln, ln - HY_FREQS).astype(F32)
    ang = (2.0 * math.pi * t) * fr + jnp.where(ln <= HY_FREQS, 0.5 * math.pi, 0.0)
    feats = jnp.where(ln == 0, t, jnp.where(ln <= 2 * HY_FREQS, jnp.sin(ang), 0.0))
    h = jnp.sin(HY_SIN_FREQ * (_dot3(feats, w1) + b1))
    h = jnp.sin(HY_SIN_FREQ * (_dot3(h, w2) + b2))
    h_cur = h[HALO:]
    h_prev = pltpu.roll(h, 1, 0)[HALO:]
    n = (n0 + lax.broadcasted_iota(jnp.int32, (rows, 1), 0)).astype(F32)
    t_row = n * inv_len
    taps = lambda hh, w3: jnp.concatenate([_dot3(hh, w3[0]), _dot3(hh, w3[1])], axis=0)
    hf = taps(h_cur, w3f) * jnp.exp(-t_row * jnp.abs(dec_f))
    hb = taps(h_prev, w3b) * jnp.exp(-(t_row - inv_len) * jnp.abs(dec_b))
    hb = jnp.where(n >= 1.0, hb, 0.0)
    return hf, hb


def _filter_kernel(w1_ref, b1_ref, w2_ref, b2_ref, w3f_ref, w3b_ref, df_ref, db_ref, o_ref, *, rows):
    hf, hb = _filter_block(pl.program_id(0) * rows, rows, 1.0 / L, w1_ref[...], b1_ref[...], w2_ref[...],
                           b2_ref[...], w3f_ref[...], w3b_ref[...], df_ref[...], db_ref[...])
    o_ref[:, 0:256] = hf[:, 0:256]
    o_ref[:, 256:512] = hb[:, 0:256]
    o_ref[:, 512:768] = hf[:, 256:512]
    o_ref[:, 768:1024] = hb[:, 256:512]


def _filter_gen(fp):
    rows = 512
    return pl.pallas_call(
        functools.partial(_filter_kernel, rows=rows),
        grid=(L // rows,),
        in_specs=[_const_spec((128, 128)), _const_spec((1, 128)), _const_spec((128, 128)),
                  _const_spec((1, 128)), _const_spec((2, 128, 512)), _const_spec((2, 128, 512)),
                  _const_spec((1, 512)), _const_spec((1, 512))],
        out_specs=pl.BlockSpec((rows, 1024), lambda i: (i, 0)),
        out_shape=jax.ShapeDtypeStruct((L, 1024), F32),
        compiler_params=_params("parallel"),
        name="hyena_filter",
    )(*fp)


def _fft_fwd1_kernel(x_ref, m_ref, o_ref):
    x2 = x_ref.reshape(FFT_R * FFT_TJ, 128)
    o2 = o_ref.reshape(2 * FFT_K1 * FFT_TJ, 128)
    for jj in range(FFT_TJ):
        xj = x2[pl.ds(jj, FFT_R, stride=FFT_TJ), :]
        res = jnp.dot(m_ref[...], xj.astype(BF16), preferred_element_type=F32)
        o2[pl.ds(jj, FFT_K1, stride=FFT_TJ), :] = res[:FFT_K1]
        o2[pl.ds(FFT_K1 * FFT_TJ + jj, FFT_K1, stride=FFT_TJ), :] = res[FFT_K1:]


def _fft_fwd1(x3, m1):
    c = x3.shape[2]
    return pl.pallas_call(
        _fft_fwd1_kernel,
        grid=(FFT_N2 // FFT_TJ, c // 128),
        in_specs=[pl.BlockSpec((FFT_R, FFT_TJ, 128), lambda j, cc: (0, j, cc)), _const_spec((FFT_N1, FFT_R))],
        out_specs=pl.BlockSpec((2, FFT_K1, FFT_TJ, 128), lambda j, cc: (0, 0, j, cc)),
        out_shape=jax.ShapeDtypeStruct((2, FFT_K1, FFT_N2, c), F32),
        compiler_params=_params("parallel", "parallel"),
        name="fft_fwd1",
    )(x3, m1)


def _filt_spec_kernel(a_ref, w_ref, o_ref):
    for j in range(FFT_TK):
        a = jnp.concatenate([a_ref[0, j], a_ref[1, j]], axis=0).astype(BF16)
        x = jnp.dot(w_ref[j], a, preferred_element_type=F32)
        o_ref[0, 0, j] = x[:FFT_N2, 0:256] + x[:FFT_N2, 256:512]
        o_ref[0, 1, j] = x[FFT_N2:, 0:256] - x[FFT_N2:, 256:512]


def _filt_spec(fa, w):
    return pl.pallas_call(
        _filt_spec_kernel,
        grid=(HY_ORDER, FFT_K1 // FFT_TK),
        in_specs=[pl.BlockSpec((2, FFT_TK, FFT_N2, 512), lambda o, i: (0, i, 0, o)),
                  pl.BlockSpec((FFT_TK, 256, 256), lambda o, i: (i, 0, 0))],
        out_specs=pl.BlockSpec((1, 2, FFT_TK, FFT_N2, 256), lambda o, i: (o, 0, i, 0, 0)),
        out_shape=jax.ShapeDtypeStruct((HY_ORDER, 2, FFT_K1, FFT_N2, 256), F32),
        compiler_params=_params("parallel", "parallel"),
        name="hyena_filter_spectrum",
    )(fa, w)


def _fft_mid_kernel(a_ref, w_ref, wt_ref, k_ref, o_ref):
    for j in range(FFT_TK):
        a = jnp.concatenate([a_ref[0, j], a_ref[1, j]], axis=0).astype(BF16)
        x = jnp.dot(w_ref[j], a, preferred_element_type=F32)
        xr, xi = x[:FFT_N2], x[FFT_N2:]
        kr, ki = k_ref[0, 0, j], k_ref[0, 1, j]
        y = jnp.concatenate([xr * kr - xi * ki, xr * ki + xi * kr], axis=0).astype(BF16)
        b = jnp.dot(wt_ref[j], y, preferred_element_type=F32)
        o_ref[0, j] = b[:FFT_N2]
        o_ref[1, j] = b[FFT_N2:]


def _fft_mid(a, w, wt, kf, order):
    blk = pl.BlockSpec((2, FFT_TK, FFT_N2, 256), lambda i: (0, i, 0, 0))
    wblk = pl.BlockSpec((FFT_TK, 256, 256), lambda i: (i, 0, 0))
    return pl.pallas_call(
        _fft_mid_kernel,
        grid=(FFT_K1 // FFT_TK,),
        in_specs=[blk, wblk, wblk,
                  pl.BlockSpec((1, 2, FFT_TK, FFT_N2, 256), lambda i: (order, 0, i, 0, 0))],
        out_specs=blk,
        out_shape=jax.ShapeDtypeStruct((2, FFT_K1, FFT_N2, 256), F32),
        compiler_params=_params("parallel"),
        name="fft_mid",
    )(a, w, wt, kf)


def _fft_inv1_kernel(b_ref, m_ref, x_ref, z_ref, fb_ref, o_ref):
    b2 = b_ref.reshape(2 * FFT_K1 * FFT_TJ, 128)
    x2 = x_ref.reshape(FFT_R * FFT_TJ, 128)
    z2 = z_ref.reshape(FFT_R * FFT_TJ, 128)
    o2 = o_ref.reshape(FFT_R * FFT_TJ, 128)
    for jj in range(FFT_TJ):
        b = jnp.concatenate([b2[pl.ds(jj, FFT_K1, stride=FFT_TJ), :],
                             b2[pl.ds(FFT_K1 * FFT_TJ + jj, FFT_K1, stride=FFT_TJ), :]], axis=0).astype(BF16)
        y = jnp.dot(m_ref[...], b, preferred_element_type=F32)
        rows = pl.ds(jj, FFT_R, stride=FFT_TJ)
        o2[rows, :] = x2[rows, :] * (y + fb_ref[...] * z2[rows, :])


def _fft_inv1(b4, minv, gate3, z3, fb):
    sig = pl.BlockSpec((FFT_R, FFT_TJ, 128), lambda j, cc: (0, j, cc))
    return pl.pallas_call(
        _fft_inv1_kernel,
        grid=(FFT_N2 // FFT_TJ, 2),
        in_specs=[pl.BlockSpec((2, FFT_K1, FFT_TJ, 128), lambda j, cc: (0, 0, j, cc)),
                  _const_spec((FFT_R, FFT_N1)), sig, sig, pl.BlockSpec((1, 128), lambda j, cc: (0, cc))],
        out_specs=sig,
        out_shape=jax.ShapeDtypeStruct((FFT_R, FFT_N2, 256), F32),
        compiler_params=_params("parallel", "parallel"),
        name="fft_inv1_gate",
    )(b4, minv, gate3, z3, fb)


def _ctx_hyena_kernel(a_ref, cw_ref, cb_ref, w1_ref, b1_ref, w2_ref, b2_ref, w3f_ref, w3b_ref, df_ref, db_ref,
                      fb_ref, fc_ref, fi_ref, o_ref):
    zero = jnp.zeros((HALO, 768), F32)
    n = LC + 2 * HALO
    ea = jnp.concatenate([zero, a_ref[...], zero], axis=0)
    w = cw_ref[...]
    y = (cb_ref[...] + w[0:1] * pltpu.roll(ea, 1, 0)[HALO:HALO + LC] + w[1:2] * ea[HALO:HALO + LC]
         + w[2:3] * pltpu.roll(ea, n - 1, 0)[HALO:HALO + LC])
    hf, hb = _filter_block(0, LC, 1.0 / LC, w1_ref[...], b1_ref[...], w2_ref[...], b2_ref[...],
                           w3f_ref[...], w3b_ref[...], df_ref[...], db_ref[...])
    fc = fc_ref[...]
    fi = fi_ref[...]
    sf = _dot3(fc, hf)
    sb = _dot3(fc, hb)
    kr = sf[:LC] + sb[:LC]
    ki = sf[LC:] - sb[LC:]
    z = y[:, 0:256]
    fb = fb_ref[...]
    for o in range(HY_ORDER):
        s = _dot3(fc, z)
        sr, si = s[:LC], s[LC:]
        c0 = 256 * o
        yr = sr * kr[:, c0:c0 + 256] - si * ki[:, c0:c0 + 256]
        yi = sr * ki[:, c0:c0 + 256] + si * kr[:, c0:c0 + 256]
        conv = _dot3(fi, jnp.concatenate([yr, yi], axis=0))
        z = y[:, 256 * (o + 1):256 * (o + 2)] * (conv + fb[o:o + 1] * z)
    o_ref[...] = z


def _ctx_hyena(a, conv_w, conv_b, fp, fbias, fc, fi):
    return pl.pallas_call(
        _ctx_hyena_kernel,
        grid=(1,),
        in_specs=[pl.BlockSpec((LC, 768), lambda i: (L // LC, 0)),
                  _const_spec((3, 768)), _const_spec((1, 768)),
                  _const_spec((128, 128)), _const_spec((1, 128)), _const_spec((128, 128)),
                  _const_spec((1, 128)), _const_spec((2, 128, 512)), _const_spec((2, 128, 512)),
                  _const_spec((1, 512)), _const_spec((1, 512)),
                  _const_spec((HY_ORDER, 256)), _const_spec((2 * LC, LC)), _const_spec((LC, 2 * LC))],
        out_specs=pl.BlockSpec((LC, 256), lambda i: (0, 0)),
        out_shape=jax.ShapeDtypeStruct((LC, 256), F32),
        compiler_params=_params("arbitrary"),
        name="hyena_ctx",
    )(a, conv_w, conv_b, *fp, fbias, fc, fi)


def _s5_in_kernel(ua_ref, ub_ref, w_ref, o_ref):
    u = jnp.concatenate([r[pl.ds(j, S5_RB, stride=S5_T), :].astype(BF16)
                         for j in range(S5_T) for r in (ua_ref, ub_ref)], axis=1)
    o_ref[0] = jnp.dot(u, w_ref[0], preferred_element_type=F32)


def _s5_in(u2, w3):
    return pl.pallas_call(
        _s5_in_kernel,
        grid=(3, S5_ROWS // S5_RB),
        in_specs=[pl.BlockSpec((S5_RB * S5_T, DG // 2), lambda m, r: (r, 0)),
                  pl.BlockSpec((S5_RB * S5_T, DG // 2), lambda m, r: (r, 1)),
                  pl.BlockSpec((1, S5_W, S5_W), lambda m, r: (m, 0, 0))],
        out_specs=pl.BlockSpec((1, S5_RB, S5_W), lambda m, r: (m, r, 0)),
        out_shape=jax.ShapeDtypeStruct((3, S5_ROWS, S5_W), F32),
        compiler_params=_params("parallel", "parallel"),
        name="s5_chunk_in",
    )(u2, u2, w3)


def _s5_scan_kernel(f_ref, b_ref, tab_ref, hf_ref, hb_ref, sf, sb, bf, bb):
    j = pl.program_id(0)

    @pl.when(j == 0)
    def _():
        sf[...] = jnp.zeros_like(sf)
        sb[...] = jnp.zeros_like(sb)

    half = S5_W // 2
    row = lax.broadcasted_iota(jnp.int32, (8, S5_W), 0)

    def cmul(a, x):
        ar, ai = a[:, :half], a[:, half:]
        xr, xi = x[:, :half], x[:, half:]
        return jnp.concatenate([ar * xr - ai * xi, ar * xi + ai * xr], axis=1)

    def shifted(x, k, down):
        if down:
            return jnp.where(row >= k, pltpu.roll(x, k, 0), 0.0)
        return jnp.where(row < 8 - k, pltpu.roll(x, 8 - k, 0), 0.0)

    def tile_scan(x, carry, d, down):
        tab = tab_ref[d]
        y = x
        for n, k in enumerate((1, 2, 4)):
            y = y + cmul(tab[8 + n:9 + n], shifted(y, k, down))
        state = y + cmul(tab[0:8], carry)
        edge = 7 if down else 0
        new_carry = jnp.broadcast_to(state[edge:edge + 1], (8, S5_W))
        before = shifted(state, 1, down)
        first = 0 if down else 7
        return jnp.where(row == first, carry, before), new_carry

    cf = sf[...]
    cb = sb[...]
    for tt in range(S5_SB // 8):
        lo = 8 * tt
        tile, cf = tile_scan(f_ref[0, lo:lo + 8, :], cf, 0, True)
        bf[lo:lo + 8, :] = tile
        lo = S5_SB - 8 - 8 * tt
        tile, cb = tile_scan(b_ref[0, lo:lo + 8, :], cb, 1, False)
        bb[lo:lo + 8, :] = tile
    sf[...] = cf
    sb[...] = cb
    hf_ref[...] = bf[...].astype(BF16)
    hb_ref[...] = bb[...].astype(BF16)


def _s5_scan(fb, tab):
    fidx = lambda j: jnp.where(j == 0, S5_NSB - 1, j - 1)
    bidx = lambda j: S5_NSB - 1 - j
    o = jax.ShapeDtypeStruct((S5_ROWS, S5_W), BF16)
    return pl.pallas_call(
        _s5_scan_kernel,
        grid=(S5_NSB,),
        in_specs=[pl.BlockSpec((1, S5_SB, S5_W), lambda j: (1, fidx(j), 0)),
                  pl.BlockSpec((1, S5_SB, S5_W), lambda j: (2, bidx(j), 0)),
                  _const_spec((2, 16, S5_W))],
        out_specs=[pl.BlockSpec((S5_SB, S5_W), lambda j: (fidx(j), 0)),
                   pl.BlockSpec((S5_SB, S5_W), lambda j: (bidx(j), 0))],
        out_shape=[o, o],
        scratch_shapes=[pltpu.VMEM((8, S5_W), F32), pltpu.VMEM((8, S5_W), F32),
                        pltpu.VMEM((S5_SB, S5_W), F32), pltpu.VMEM((S5_SB, S5_W), F32)],
        compiler_params=_params("arbitrary"),
        name="s5_scan",
    )(fb, fb, tab)


def _s5_out_kernel(y_ref, hf_ref, hb_ref, wf_ref, wb_ref, o_ref, y_sc):
    half = pl.program_id(1)

    @pl.when(half == 0)
    def _():
        y_sc[...] = (y_ref[0] + jnp.dot(hf_ref[...], wf_ref[...], preferred_element_type=F32)
                     + jnp.dot(hb_ref[...], wb_ref[...], preferred_element_type=F32))

    for hh in range(2):
        @pl.when(half == hh)
        def _():
            for j in range(S5_T):
                c0 = DG * j + 128 * hh
                o_ref[pl.ds(j, S5_RB, stride=S5_T), :] = y_sc[:, c0:c0 + 128]


def _s5_out(fb, hf, hb, wf, wb):
    rows = pl.BlockSpec((S5_RB, S5_W), lambda r, c: (r, 0))
    return pl.pallas_call(
        _s5_out_kernel,
        grid=(S5_ROWS // S5_RB, 2),
        in_specs=[pl.BlockSpec((1, S5_RB, S5_W), lambda r, c: (0, r, 0)), rows, rows,
                  _const_spec((S5_W, S5_W)), _const_spec((S5_W, S5_W))],
        out_specs=pl.BlockSpec((S5_RB * S5_T, 128), lambda r, c: (r, c)),
        out_shape=jax.ShapeDtypeStruct((LT, DG), F32),
        scratch_shapes=[pltpu.VMEM((S5_RB, S5_W), F32)],
        compiler_params=_params("parallel", "arbitrary"),
        name="s5_chunk_out",
    )(fb, hf, hb, wf, wb)


def _value_rows(vt):
    return jnp.concatenate([vt, jnp.ones((ATT_VE - HEAD, vt.shape[1]), BF16)], axis=0)


def _attend(k, ve, q_ref, m_sc, acc_sc):
    ks = min(ATT_KS, k.shape[0])
    units = [(c, g) for c in range(k.shape[0] // ks) for g in range(ATT_NG)]

    def scores(unit):
        c, g = unit
        j, c0 = divmod(g * ATT_GQ, ATT_TQ)
        return jnp.dot(k[c * ks:(c + 1) * ks], q_ref[HEAD * j:HEAD * (j + 1), c0:c0 + ATT_GQ],
                       preferred_element_type=F32)

    s_next = scores(units[0])
    for u, (c, g) in enumerate(units):
        s = s_next
        if u + 1 < len(units):
            s_next = scores(units[u + 1])
        m_prev = m_sc[g]
        m_new = jnp.maximum(m_prev, jnp.max(s, axis=0, keepdims=True))
        alpha = jnp.exp2(m_prev - m_new)
        p = jnp.exp2(s - m_new).astype(BF16)
        acc_sc[g] = alpha * acc_sc[g] + jnp.dot(ve[:, c * ks:(c + 1) * ks], p, preferred_element_type=F32)
        m_sc[g] = m_new


def _flash_kernel(q_ref, k_ref, v_ref, kc_ref, vc_ref, o_ref, m_sc, acc_sc):
    ki = pl.program_id(2)

    @pl.when(ki == 0)
    def _():
        m_sc[...] = jnp.full_like(m_sc, -1e30)
        acc_sc[...] = jnp.zeros_like(acc_sc)

    _attend(k_ref[0], _value_rows(v_ref[...]), q_ref, m_sc, acc_sc)

    @pl.when(ki == pl.num_programs(2) - 1)
    def _():
        _attend(kc_ref[0], _value_rows(vc_ref[...]), q_ref, m_sc, acc_sc)
        for g in range(ATT_NG):
            j, c0 = divmod(g * ATT_GQ, ATT_TQ)
            acc = acc_sc[g]
            o_ref[HEAD * j:HEAD * (j + 1), c0:c0 + ATT_GQ] = (acc[:HEAD] / acc[HEAD:HEAD + 1]).astype(BF16)


def _flash(qt, k, vt):
    return pl.pallas_call(
        _flash_kernel,
        grid=(KVH, L // ATT_TQ, L // ATT_TK),
        in_specs=[pl.BlockSpec((2 * HEAD, ATT_TQ), lambda h, qi, ki: (h, qi)),
                  pl.BlockSpec((1, ATT_TK, HEAD), lambda h, qi, ki: (h, ki, 0)),
                  pl.BlockSpec((HEAD, ATT_TK), lambda h, qi, ki: (h, ki)),
                  pl.BlockSpec((1, LC, HEAD), lambda h, qi, ki: (h, L // LC, 0)),
                  pl.BlockSpec((HEAD, LC), lambda h, qi, ki: (h, L // LC))],
        out_specs=pl.BlockSpec((2 * HEAD, ATT_TQ), lambda h, qi, ki: (h, qi)),
        out_shape=jax.ShapeDtypeStruct((QH * HEAD, L), BF16),
        scratch_shapes=[pltpu.VMEM((ATT_NG, 1, ATT_GQ), F32), pltpu.VMEM((ATT_NG, ATT_VE, ATT_GQ), F32)],
        compiler_params=_params("parallel", "parallel", "arbitrary"),
        name="flash_attention",
    )(qt, k, vt, k, vt)


def _ctx_attn_kernel(q_ref, k_ref, v_ref, o_ref):
    ve = _value_rows(v_ref[...])
    for j in range(2):
        s = jnp.dot(k_ref[0], q_ref[HEAD * j:HEAD * (j + 1), :], preferred_element_type=F32)
        p = jnp.exp2(s - jnp.max(s, axis=0, keepdims=True)).astype(BF16)
        acc = jnp.dot(ve, p, preferred_element_type=F32)
        o_ref[HEAD * j:HEAD * (j + 1), :] = (acc[:HEAD] / acc[HEAD:HEAD + 1]).astype(BF16)


def _ctx_attn(qt, k, vt):
    return pl.pallas_call(
        _ctx_attn_kernel,
        grid=(KVH,),
        in_specs=[pl.BlockSpec((2 * HEAD, LC), lambda h: (h, L // LC)),
                  pl.BlockSpec((1, LC, HEAD), lambda h: (h, L // LC, 0)),
                  pl.BlockSpec((HEAD, LC), lambda h: (h, L // LC))],
        out_specs=pl.BlockSpec((2 * HEAD, LC), lambda h: (h, 0)),
        out_shape=jax.ShapeDtypeStruct((QH * HEAD, LC), BF16),
        compiler_params=_params("parallel"),
        name="ctx_attention",
    )(qt, k, vt)


def _gelu_tanh(x):
    return 0.5 * x * (1.0 + jnp.tanh(math.sqrt(2.0 / math.pi) * (x + 0.044715 * (x * x * x))))


def _post_kernel(x_ref, mod_ref, hyl_ref, hyc_ref, s5_ref, po_ref, atl_ref, atc_ref,
                 gw_ref, gb_ref, wo_ref, g1_ref, g2_ref, g3_ref, w1_ref, w2_ref, o_ref):
    is_ctx = pl.program_id(0) >= NLB
    m = mod_ref[0]
    g = _gelu_tanh(s5_ref[...])
    s5o = g * jax.nn.sigmoid(_bdot(g, gw_ref[...]) + gb_ref[...])
    hy = jnp.where(is_ctx, hyc_ref[...], hyl_ref[...])
    at = jnp.where(is_ctx, atc_ref[...], atl_ref[...]).astype(F32).T
    o = (_bdot(hy, wo_ref[0:256, :]) + _bdot(s5o, wo_ref[256:512, :]) + _bdot(po_ref[...], wo_ref[512:768, :])
         + _bdot(at, wo_ref[768:1024, :]))
    x = x_ref[...] + m[2:3] * _rms(o, g1_ref[...])
    h = _rms(x, g2_ref[...]) * (1.0 + m[4:5]) + m[3:4]
    f = jnp.dot(h.astype(BF16), w1_ref[...], preferred_element_type=F32)
    f = jnp.square(jnp.maximum(f, 0.0)).astype(BF16)
    f = jnp.dot(f, w2_ref[...], preferred_element_type=F32)
    o_ref[...] = x + m[5:6] * _rms(f, g3_ref[...])


def _post(x, mods, hy_l, hy_c, s5y, po, at_l, at_c, glu_w, glu_b, w_out, g1, g2, g3, w1, w2, n_blocks):
    row = lambda w: pl.BlockSpec((TR, w), lambda i: (i, 0))
    lat = lambda i: jnp.minimum(i, NLB - 1)
    return pl.pallas_call(
        _post_kernel,
        grid=(n_blocks,),
        in_specs=[row(D),
                  pl.BlockSpec((1, N_MOD, D), lambda i: (i // NLB, 0, 0)),
                  pl.BlockSpec((TR, 256), lambda i: (lat(i), 0)),
                  _const_spec((LC, 256)),
                  row(256), row(256),
                  pl.BlockSpec((QH * HEAD, TR), lambda i: (0, lat(i))),
                  _const_spec((QH * HEAD, LC)),
                  _const_spec((256, 256)), _const_spec((1, 256)), _const_spec((D, D)),
                  _const_spec((1, D)), _const_spec((1, D)), _const_spec((1, D)),
                  _const_spec((D, D_FF)), _const_spec((D_FF, D))],
        out_specs=row(D),
        out_shape=jax.ShapeDtypeStruct((n_blocks * TR, D), F32),
        compiler_params=_params("parallel"),
        name="post_mix_mlp",
    )(x, mods, hy_l, hy_c, s5y, po, at_l, at_c, glu_w, glu_b, w_out, g1, g2, g3, w1, w2)


def _dft_tables():
    n = jnp.arange(128, dtype=jnp.int32)
    m = ((2 * n[:, None] + 1) * n[None, :]) % (2 * FFT_N1)
    ang = m.astype(F32) * (2.0 * math.pi / (2 * FFT_N1))
    c1, s1 = jnp.cos(ang), jnp.sin(ang)
    m1 = jnp.concatenate([c1, -s1], axis=0)
    m1inv = jnp.concatenate([c1.T, -s1.T], axis=1) * (2.0 / FFT_N)
    kk = n[:, None, None] + FFT_N1 * n[None, :, None]
    mm = ((2 * kk + 1) * n[None, None, :]) % (2 * FFT_N)
    phi = mm.astype(F32) * (2.0 * math.pi / (2 * FFT_N))
    cm, sm = jnp.cos(phi), jnp.sin(phi)
    w = jnp.concatenate([jnp.concatenate([cm, sm], axis=2), jnp.concatenate([-sm, cm], axis=2)], axis=1)
    wt = jnp.swapaxes(w, 1, 2)
    nc = jnp.arange(LC, dtype=jnp.int32)
    mc = ((2 * nc[:, None] + 1) * nc[None, :]) % (4 * LC)
    angc = mc.astype(F32) * (2.0 * math.pi / (4 * LC))
    cc, sc = jnp.cos(angc), jnp.sin(angc)
    fc = jnp.concatenate([cc, -sc], axis=0)
    fi = jnp.concatenate([cc.T, -sc.T], axis=1) * (2.0 / (2 * LC))
    return m1.astype(BF16), m1inv.astype(BF16), w.astype(BF16), wt.astype(BF16), fc, fi


def _rope_tables():
    t = jnp.arange(L, dtype=jnp.int32)
    inv = ROPE_THETA ** (-jnp.arange(0, HEAD // 2, 2, dtype=F32) / (HEAD // 2))
    ang = jnp.concatenate([(t // GRID_W).astype(F32)[:, None] * inv[None, :],
                           (t % GRID_W).astype(F32)[:, None] * inv[None, :]], axis=-1)
    ang = jnp.concatenate([ang, jnp.zeros((LC, HEAD // 2), F32)], axis=0)
    c, s = jnp.cos(ang), jnp.sin(ang)
    return jnp.concatenate([c, c, c, c], axis=1), jnp.concatenate([-s, s, -s, s], axis=1)


def _s5_tables(a_re, a_im, log_dt, b_re, b_im, c_re, c_im, d):
    dt = jnp.exp(log_dt)[..., None]
    tau = jnp.arange(S5_T + 1, dtype=F32)[:, None, None, None]
    mag = jnp.exp(a_re * dt * tau)
    pr, pi = mag * jnp.cos(a_im * dt * tau), mag * jnp.sin(a_im * dt * tau)
    lam_re, lam_im = pr[1], pi[1]
    den = a_re * a_re + a_im * a_im
    nr, ni = lam_re - 1.0, lam_im
    cr = (nr * a_re + ni * a_im) / den
    ci = (ni * a_re - nr * a_im) / den
    bb_re = cr[..., None] * b_re - ci[..., None] * b_im
    bb_im = cr[..., None] * b_im + ci[..., None] * b_re
    cl_re = c_re[None] * pr[:, :, :, None, :] - c_im[None] * pi[:, :, :, None, :]
    cl_im = c_re[None] * pi[:, :, :, None, :] + c_im[None] * pr[:, :, :, None, :]
    kk = jnp.einsum('tdghp,dgpk->tdghk', cl_re, bb_re) - jnp.einsum('tdghp,dgpk->tdghk', cl_im, bb_im)
    k0 = kk[0, 0] + kk[0, 1] + d.reshape(S5_G, S5_H)[:, :, None] * jnp.eye(S5_H, dtype=F32)[None]
    kfull = jnp.concatenate([kk[1:S5_T, 1][::-1], k0[None], kk[1:S5_T, 0]], axis=0)
    st = jnp.arange(S5_T)
    kt = kfull[st[None, :] - st[:, None] + S5_T - 1]
    th = S5_T * S5_H
    expand = jnp.einsum('tu,hk->thuk', jnp.eye(S5_T, dtype=F32), jnp.eye(S5_H, dtype=F32))
    expand = jnp.broadcast_to(expand[:, :, :, None, :], (S5_T, S5_H, S5_T, S5_G, S5_H)).reshape(th, S5_W)
    col_g = (jnp.arange(S5_W) // S5_H) % S5_G
    row_g_u = (jnp.arange(S5_W) // S5_H) % S5_G
    row_g_st = (jnp.arange(S5_W) // S5_P) % S5_G
    st_col_g = jnp.arange(S5_G * S5_P) // S5_P

    def widen(compact, row_g):
        full = jnp.dot(compact.astype(BF16), expand.astype(BF16), preferred_element_type=F32)
        return jnp.where(row_g[:, None] == col_g[None, :], full, 0.0).astype(BF16)

    m_intra = widen(kt.transpose(0, 2, 4, 1, 3).reshape(S5_W, th), row_g_u)

    def state_in(pw_re, pw_im, d_):
        re = pw_re[:, :, :, None] * bb_re[d_][None] - pw_im[:, :, :, None] * bb_im[d_][None]
        im = pw_re[:, :, :, None] * bb_im[d_][None] + pw_im[:, :, :, None] * bb_re[d_][None]

        def wide(x):
            c = x.transpose(0, 1, 3, 2).reshape(S5_W, S5_P)
            return jnp.where(row_g_u[:, None] == st_col_g[None, :], jnp.tile(c, (1, S5_G)), 0.0)
        return jnp.concatenate([wide(re), wide(im)], axis=1).astype(BF16)

    m_fst = state_in(pr[:S5_T, 0][::-1], pi[:S5_T, 0][::-1], 0)
    m_bst = state_in(pr[:S5_T, 1], pi[:S5_T, 1], 1)

    def state_out(cre, cim):
        both = jnp.stack([cre, -cim], axis=0)
        return widen(both.transpose(0, 2, 4, 1, 3).reshape(S5_W, th), row_g_st)

    m_fout = state_out(cl_re[1:, 0], cl_im[1:, 0])
    m_bout = state_out(cl_re[1:, 1][::-1], cl_im[1:, 1][::-1])
    mm = (S5_T * jnp.arange(1, 9, dtype=F32))[:, None, None, None]
    mg = jnp.exp(a_re * dt * mm)
    qr = (mg * jnp.cos(a_im * dt * mm)).reshape(8, 2, -1)
    qi = (mg * jnp.sin(a_im * dt * mm)).reshape(8, 2, -1)
    q = jnp.concatenate([qr, qi], axis=2)
    zero = jnp.zeros((5, S5_W), F32)
    steps = jnp.stack([q[0], q[1], q[3]], axis=0)
    lam8 = jnp.stack([jnp.concatenate([q[:, 0], steps[:, 0], zero], axis=0),
                      jnp.concatenate([q[::-1, 1], steps[:, 1], zero], axis=0)], axis=0)
    return jnp.stack([m_intra, m_fst, m_bst], axis=0), lam8, m_fout, m_bout


def _filter_params(w1, b1, w2, b2, w3, decay):
    zf = jnp.zeros((HY_FFN, HY_FFN), F32)
    w1p = jnp.concatenate([w1, jnp.zeros((HY_FFN - w1.shape[0], HY_FFN), F32)], axis=0)
    pair = lambda w: jnp.concatenate([jnp.concatenate([w, zf], axis=1), jnp.concatenate([zf, w], axis=1)], axis=0)
    two = lambda b: jnp.concatenate([b, b]).reshape(1, 2 * HY_FFN)
    w3r = w3.reshape(HY_FFN, HY_ORDER, 2, DG)
    z3 = jnp.zeros((HY_FFN, 512), F32)
    halves = lambda w: jnp.stack([jnp.concatenate([w, z3], axis=0), jnp.concatenate([z3, w], axis=0)], axis=0)
    return (pair(w1p), two(b1), pair(w2), two(b2),
            halves(w3r[:, :, 0].reshape(HY_FFN, 512)), halves(w3r[:, :, 1].reshape(HY_FFN, 512)),
            decay[:, 0].reshape(1, 512), decay[:, 1].reshape(1, 512))


def _block_diag(w):
    g, n, _ = w.shape
    return jnp.einsum('gcd,gj->gcjd', w, jnp.eye(g, dtype=w.dtype)).reshape(g * n, g * n)


def kernel(x, c, ctx, c_ctx, mod_w, mod_b, norm_pre_mix, norm_post_mix, norm_pre_mlp, norm_post_mlp,
           w_in, w_out, hy_conv_w, hy_conv_b, hy_ffn_w1, hy_ffn_b1, hy_ffn_w2, hy_ffn_b2, hy_ffn_w3,
           hy_decay, hy_bias, s5_a_re, s5_a_im, s5_log_dt, s5_b_re, s5_b_im, s5_c_re, s5_c_im, s5_d,
           s5_glu_w, s5_glu_b, pool_w, pool_scale, att_q_norm, att_k_norm, mlp_w1, mlp_w2):
    xs = jnp.concatenate([x[0], ctx[0]], axis=0)
    mods = _modulation(c, c_ctx, mod_w, mod_b)
    m1, m1inv, wk, wkt, fc, fi = _dft_tables()
    cs, sn = _rope_tables()
    perm = jnp.concatenate([jnp.arange(0, HEAD, 2), jnp.arange(1, HEAD, 2)])
    qcols = 1280 + (jnp.arange(QH)[:, None] * HEAD + perm[None, :]).reshape(-1)
    kcols = 1536 + (jnp.arange(KVH)[:, None] * HEAD + perm[None, :]).reshape(-1)
    cols = jnp.concatenate([jnp.arange(1280), qcols, kcols, jnp.arange(1664, IN_COLS)])
    head_mean = _block_diag(jnp.full((QH, HEAD, HEAD), 1.0 / HEAD, F32)).astype(BF16)

    for i in range(DEPTH):
        w_in_i = w_in[i][:, cols].astype(BF16)
        qg = jnp.tile(att_q_norm[i][perm], QH).reshape(1, 256)
        kg = jnp.tile(att_k_norm[i][perm], KVH).reshape(1, 128)
        a, s, p, q, k, v = _premix(xs, mods[i], norm_pre_mix[i].reshape(1, D), w_in_i, cs, sn, qg, kg, head_mean)

        vv, x1, x2, po = _local(a, p, hy_conv_w[i], hy_conv_b[i].reshape(1, 768),
                                _block_diag(pool_w[i]), pool_scale[i].reshape(1, 256))

        fp = _filter_params(hy_ffn_w1[i], hy_ffn_b1[i], hy_ffn_w2[i], hy_ffn_b2[i], hy_ffn_w3[i], hy_decay[i])
        filt = _filter_gen(fp)
        fa = _fft_fwd1(filt.reshape(FFT_R, FFT_N2, 1024), m1)
        kf = _filt_spec(fa, wk)
        time_major = lambda t: t.reshape(LT // FFT_N2, FFT_N2, 256)
        z = time_major(vv)
        gates = (time_major(x1), time_major(x2))
        for o in range(HY_ORDER):
            fa_z = _fft_fwd1(z, m1)
            bm = _fft_mid(fa_z, wk, wkt, kf, o)
            z = _fft_inv1(bm, m1inv, gates[o], z, hy_bias[i][o].reshape(1, 256))
        hy_l = z.reshape(L, 256)
        hy_c = _ctx_hyena(a, hy_conv_w[i], hy_conv_b[i].reshape(1, 768), fp, hy_bias[i], fc, fi)

        w3, lam8, m_fout, m_bout = _s5_tables(s5_a_re[i], s5_a_im[i], s5_log_dt[i], s5_b_re[i], s5_b_im[i],
                                              s5_c_re[i], s5_c_im[i], s5_d[i])
        fb = _s5_in(s, w3)
        hf, hb = _s5_scan(fb, lam8)
        s5y = _s5_out(fb, hf, hb, m_fout, m_bout)

        at_l = _flash(q, k, v)
        at_c = _ctx_attn(q, k, v)

        xs = _post(xs, mods[i], hy_l, hy_c, s5y, po, at_l, at_c,
                   s5_glu_w[i].astype(BF16), s5_glu_b[i].reshape(1, 256), w_out[i].astype(BF16),
                   norm_post_mix[i].reshape(1, D), norm_pre_mlp[i].reshape(1, D), norm_post_mlp[i].reshape(1, D),
                   mlp_w1[i].astype(BF16), mlp_w2[i].astype(BF16),
                   NRB if i < DEPTH - 1 else NLB)
    return xs[None]
```

```python
import functools
import math

import jax
import jax.numpy as jnp
from jax import lax
from jax.experimental import pallas as pl
from jax.experimental.pallas import tpu as pltpu

F32 = jnp.float32
BF16 = jnp.bfloat16

D = 1024
L = 16384
LC = 256
LT = L + LC
DEPTH = 4
GRID_W = 64
EPS = 1e-6
N_MOD = 6
DG = 256
HY_ORDER = 2
HY_FREQS = 16
HY_FFN = 64
HY_SIN_FREQ = 1.0
S5_H = 16
S5_G = 16
S5_P = 64
HEAD = 64
QH = 4
KVH = 2
ATT_SCALE = 1.0 / math.sqrt(HEAD)
ROPE_THETA = 10000.0
D_FF = 4 * D
IN_COLS = 1792

TR = 256
NLB = L // TR
NRB = LT // TR
HALO = 8

FFT_N = 2 * L
FFT_N1 = 256
FFT_N2 = 128
FFT_K1 = FFT_N1 // 2
FFT_R = L // FFT_N2
FFT_TJ = 16
FFT_TK = 16

S5_T = 8
S5_W = S5_T * DG
S5_ROWS = LT // S5_T
S5_SB = LC // S5_T
S5_NSB = S5_ROWS // S5_SB
S5_RB = 416

ATT_TQ = 4096
ATT_TK = 4096
ATT_KS = 4096
ATT_GQ = 512
ATT_NG = 2 * ATT_TQ // ATT_GQ
ATT_VE = HEAD + 16
LOG2E = math.log2(math.e)

VMEM_LIMIT = 56 * 1024 * 1024


def _params(*sem):
    return pltpu.CompilerParams(dimension_semantics=sem, vmem_limit_bytes=VMEM_LIMIT)


def _const_spec(shape):
    nd = len(shape)
    return pl.BlockSpec(shape, lambda *_: (0,) * nd, pipeline_mode=pl.Buffered(1))


def _bdot(a, b):
    return jnp.dot(a.astype(BF16), b.astype(BF16), preferred_element_type=F32)


def _split(a):
    hi = a.astype(BF16)
    lo = (a - hi.astype(F32)).astype(BF16)
    return hi, lo


def _dot3(a, b):
    ah, al = _split(a)
    bh, bl = _split(b)
    return (jnp.dot(ah, bh, preferred_element_type=F32)
            + jnp.dot(al, bh, preferred_element_type=F32)
            + jnp.dot(ah, bl, preferred_element_type=F32))


def _rms(x, g):
    return x * lax.rsqrt(jnp.mean(x * x, axis=-1, keepdims=True) + EPS) * g


def _mod_kernel(c_ref, w_ref, b_ref, o_ref):
    c = c_ref[...]
    s = c * jax.nn.sigmoid(c)
    o_ref[0] = _dot3(s, w_ref[0]) + b_ref[0]


def _modulation(c, c_ctx, mod_w, mod_b):
    cc = jnp.concatenate([c.reshape(1, D), c_ctx.reshape(1, D), jnp.zeros((6, D), F32)], axis=0)
    tn = 1536
    out = pl.pallas_call(
        _mod_kernel,
        grid=(DEPTH, N_MOD * D // tn),
        in_specs=[pl.BlockSpec((8, D), lambda i, j: (0, 0)),
                  pl.BlockSpec((1, D, tn), lambda i, j: (i, 0, j)),
                  pl.BlockSpec((1, 1, tn), lambda i, j: (i, 0, j))],
        out_specs=pl.BlockSpec((1, 8, tn), lambda i, j: (i, 0, j)),
        out_shape=jax.ShapeDtypeStruct((DEPTH, 8, N_MOD * D), F32),
        compiler_params=_params("parallel", "parallel"),
        name="modulation",
    )(cc, mod_w, mod_b.reshape(DEPTH, 1, N_MOD * D))
    return out[:, :2].reshape(DEPTH, 2, N_MOD, D)


def _qknorm_rope(t, gain, cs, sn, hm):
    hi, lo = _split(t * t)
    ms = jnp.dot(hi, hm, preferred_element_type=F32) + jnp.dot(lo, hm, preferred_element_type=F32)
    tn = t * lax.rsqrt(ms + EPS) * gain
    w = t.shape[1]
    lane = lax.broadcasted_iota(jnp.int32, tn.shape, 1)
    swapped = jnp.where((lane % HEAD) < HEAD // 2, pltpu.roll(tn, w - HEAD // 2, 1), pltpu.roll(tn, HEAD // 2, 1))
    return tn * cs + swapped * sn


def _premix_kernel(x_ref, mod_ref, g_ref, w_ref, cs_ref, sn_ref, qg_ref, kg_ref, hm_ref,
                   a_ref, s_ref, p_ref, q_ref, k_ref, v_ref):
    m = mod_ref[0]
    h = _rms(x_ref[...], g_ref[...]) * (1.0 + m[1:2]) + m[0:1]
    u = jnp.dot(h.astype(BF16), w_ref[...], preferred_element_type=F32)
    a_ref[...] = u[:, 0:768]
    s_ref[...] = u[:, 768:1024]
    p_ref[...] = u[:, 1024:1280]
    cs = cs_ref[...]
    sn = sn_ref[...]
    hm = hm_ref[...]
    q = _qknorm_rope(u[:, 1280:1536], qg_ref[...], jnp.concatenate([cs, cs], axis=1),
                     jnp.concatenate([sn, sn], axis=1), hm) * (ATT_SCALE * LOG2E)
    k = _qknorm_rope(u[:, 1536:1664], kg_ref[...], cs, sn, hm[:128, :128])
    q_ref[...] = q.T.astype(BF16)
    v_ref[...] = u[:, 1664:1792].T.astype(BF16)
    for hh in range(KVH):
        k_ref[hh] = k[:, HEAD * hh:HEAD * (hh + 1)].astype(BF16)


def _premix(x, mods, g, w_in, cs, sn, qg, kg, hm):
    row = lambda w: pl.BlockSpec((TR, w), lambda i: (i, 0))
    return pl.pallas_call(
        _premix_kernel,
        grid=(NRB,),
        in_specs=[row(D),
                  pl.BlockSpec((1, N_MOD, D), lambda i: (i // NLB, 0, 0)),
                  _const_spec((1, D)), _const_spec((D, IN_COLS)),
                  row(128), row(128),
                  _const_spec((1, 256)), _const_spec((1, 128)), _const_spec((256, 256))],
        out_specs=[row(768), row(256), row(256),
                   pl.BlockSpec((QH * HEAD, TR), lambda i: (0, i)),
                   pl.BlockSpec((KVH, TR, HEAD), lambda i: (0, i, 0)),
                   pl.BlockSpec((KVH * HEAD, TR), lambda i: (0, i))],
        out_shape=[jax.ShapeDtypeStruct((LT, 768), F32),
                   jax.ShapeDtypeStruct((LT, 256), F32),
                   jax.ShapeDtypeStruct((LT, 256), F32),
                   jax.ShapeDtypeStruct((QH * HEAD, LT), BF16),
                   jax.ShapeDtypeStruct((KVH, LT, HEAD), BF16),
                   jax.ShapeDtypeStruct((KVH * HEAD, LT), BF16)],
        compiler_params=_params("parallel"),
        name="premix",
    )(x, mods, g, w_in, cs, sn, qg, kg, hm)


def _local_kernel(a_ref, ap_ref, an_ref, p_ref, pp_ref, pn_ref, cw_ref, cb_ref, pw_ref, ps_ref,
                  v_ref, x1_ref, x2_ref, po_ref):
    i = pl.program_id(0)
    is_ctx = i >= NLB
    s0 = jnp.where(is_ctx, L, 0)
    s1 = jnp.where(is_ctx, LT, L)
    n = TR + 2 * HALO
    idx = i * TR - HALO + lax.broadcasted_iota(jnp.int32, (n, 1), 0)
    valid = (idx >= s0) & (idx < s1)

    def shifted(e, d):
        return pltpu.roll(e, (-d) % n, 0)[HALO:HALO + TR] if d else e[HALO:HALO + TR]

    ea = jnp.where(valid, jnp.concatenate([ap_ref[...], a_ref[...], an_ref[...]], axis=0), 0.0)
    w = cw_ref[...]
    y = cb_ref[...] + w[0:1] * shifted(ea, -1) + w[1:2] * shifted(ea, 0) + w[2:3] * shifted(ea, 1)
    v_ref[...] = y[:, 0:256]
    x1_ref[...] = y[:, 256:512]
    x2_ref[...] = y[:, 512:768]

    e = jnp.where(valid, jnp.concatenate([pp_ref[...], p_ref[...], pn_ref[...]], axis=0), 0.0)
    w2 = e + pltpu.roll(e, 1, 0)
    w4 = pltpu.roll(w2, 1, 0) + pltpu.roll(w2, n - 1, 0)
    w8 = pltpu.roll(w4, 2, 0) + pltpu.roll(w4, n - 2, 0)
    w16 = pltpu.roll(w8, 4, 0) + pltpu.roll(w8, n - 4, 0)
    lane = lax.broadcasted_iota(jnp.int32, (TR, 256), 1)
    cut = lambda t: t[HALO:HALO + TR]
    sums = jnp.where(lane < 64, cut(w2), jnp.where(lane < 128, cut(w4), jnp.where(lane < 192, cut(w8), cut(w16))))
    half = jnp.where(lane < 64, 1, jnp.where(lane < 128, 2, jnp.where(lane < 192, 4, 8)))
    tl = i * TR + lax.broadcasted_iota(jnp.int32, (TR, 256), 0) - s0
    cnt = jnp.minimum(tl + half, s1 - s0) - jnp.maximum(tl - half, 0)
    pooled = sums / cnt.astype(F32) - p_ref[...]
    po_ref[...] = _bdot(pooled, pw_ref[...]) * ps_ref[...]


def _local(a, p, conv_w, conv_b, pool_bd, pool_scale):
    r8 = TR // HALO
    main = lambda w: pl.BlockSpec((TR, w), lambda i: (i, 0))
    prev = lambda w: pl.BlockSpec((HALO, w), lambda i: (jnp.maximum(i * r8 - 1, 0), 0))
    nxt = lambda w: pl.BlockSpec((HALO, w), lambda i: (jnp.minimum((i + 1) * r8, LT // HALO - 1), 0))
    o = jax.ShapeDtypeStruct((LT, 256), F32)
    return pl.pallas_call(
        _local_kernel,
        grid=(NRB,),
        in_specs=[main(768), prev(768), nxt(768), main(256), prev(256), nxt(256),
                  _const_spec((3, 768)), _const_spec((1, 768)), _const_spec((256, 256)), _const_spec((1, 256))],
        out_specs=[main(256)] * 4,
        out_shape=[o, o, o, o],
        compiler_params=_params("parallel"),
        name="local_ops",
    )(a, a, a, p, p, p, conv_w, conv_b, pool_bd, pool_scale)


def _filter_block(n0, rows, inv_len, w1, b1, w2, b2, w3f, w3b, dec_f, dec_b):
    half = rows // 2
    m = half + HALO
    idx = lax.broadcasted_iota(jnp.int32, (m, 128), 0)
    lane = lax.broadcasted_iota(jnp.int32, (m, 128), 1)
    ln = lane % HY_FFN
    t = (n0 - HALO + idx + jnp.where(lane < HY_FFN, 0, half)).astype(F32) * inv_len
    fr = jnp.where(ln <= HY_FREQS, ln, ln - HY_FREQS).astype(F32)
    ang = (2.0 * math.pi * t) * fr + jnp.where(ln <= HY_FREQS, 0.5 * math.pi, 0.0)
    feats = jnp.where(ln == 0, t, jnp.where(ln <= 2 * HY_FREQS, jnp.sin(ang), 0.0))
    h = jnp.sin(HY_SIN_FREQ * (_dot3(feats, w1) + b1))
    h = jnp.sin(HY_SIN_FREQ * (_dot3(h, w2) + b2))
    h_cur = h[HALO:]
    h_prev = pltpu.roll(h, 1, 0)[HALO:]
    n = (n0 + lax.broadcasted_iota(jnp.int32, (rows, 1), 0)).astype(F32)
    t_row = n * inv_len
    taps = lambda hh, w3: jnp.concatenate([_dot3(hh, w3[0]), _dot3(hh, w3[1])], axis=0)
    hf = taps(h_cur, w3f) * jnp.exp(-t_row * jnp.abs(dec_f))
    hb = taps(h_prev, w3b) * jnp.exp(-(t_row - inv_len) * jnp.abs(dec_b))
    hb = jnp.where(n >= 1.0, hb, 0.0)
    return hf, hb


def _filter_kernel(w1_ref, b1_ref, w2_ref, b2_ref, w3f_ref, w3b_ref, df_ref, db_ref, o_ref, *, rows):
    hf, hb = _filter_block(pl.program_id(0) * rows, rows, 1.0 / L, w1_ref[...], b1_ref[...], w2_ref[...],
                           b2_ref[...], w3f_ref[...], w3b_ref[...], df_ref[...], db_ref[...])
    o_ref[:, 0:256] = hf[:, 0:256]
    o_ref[:, 256:512] = hb[:, 0:256]
    o_ref[:, 512:768] = hf[:, 256:512]
    o_ref[:, 768:1024] = hb[:, 256:512]


def _filter_gen(fp):
    rows = 512
    return pl.pallas_call(
        functools.partial(_filter_kernel, rows=rows),
        grid=(L // rows,),
        in_specs=[_const_spec((128, 128)), _const_spec((1, 128)), _const_spec((128, 128)),
                  _const_spec((1, 128)), _const_spec((2, 128, 512)), _const_spec((2, 128, 512)),
                  _const_spec((1, 512)), _const_spec((1, 512))],
        out_specs=pl.BlockSpec((rows, 1024), lambda i: (i, 0)),
        out_shape=jax.ShapeDtypeStruct((L, 1024), F32),
        compiler_params=_params("parallel"),
        name="hyena_filter",
    )(*fp)


def _fft_fwd1_kernel(x_ref, m_ref, o_ref):
    x2 = x_ref.reshape(FFT_R * FFT_TJ, 128)
    o2 = o_ref.reshape(2 * FFT_K1 * FFT_TJ, 128)
    for jj in range(FFT_TJ):
        xj = x2[pl.ds(jj, FFT_R, stride=FFT_TJ), :]
        res = jnp.dot(m_ref[...], xj.astype(BF16), preferred_element_type=F32)
        o2[pl.ds(jj, FFT_K1, stride=FFT_TJ), :] = res[:FFT_K1]
        o2[pl.ds(FFT_K1 * FFT_TJ + jj, FFT_K1, stride=FFT_TJ), :] = res[FFT_K1:]


def _fft_fwd1(x3, m1):
    c = x3.shape[2]
    return pl.pallas_call(
        _fft_fwd1_kernel,
        grid=(FFT_N2 // FFT_TJ, c // 128),
        in_specs=[pl.BlockSpec((FFT_R, FFT_TJ, 128), lambda j, cc: (0, j, cc)), _const_spec((FFT_N1, FFT_R))],
        out_specs=pl.BlockSpec((2, FFT_K1, FFT_TJ, 128), lambda j, cc: (0, 0, j, cc)),
        out_shape=jax.ShapeDtypeStruct((2, FFT_K1, FFT_N2, c), F32),
        compiler_params=_params("parallel", "parallel"),
        name="fft_fwd1",
    )(x3, m1)


def _filt_spec_kernel(a_ref, w_ref, o_ref):
    for j in range(FFT_TK):
        a = jnp.concatenate([a_ref[0, j], a_ref[1, j]], axis=0).astype(BF16)
        x = jnp.dot(w_ref[j], a, preferred_element_type=F32)
        o_ref[0, 0, j] = x[:FFT_N2, 0:256] + x[:FFT_N2, 256:512]
        o_ref[0, 1, j] = x[FFT_N2:, 0:256] - x[FFT_N2:, 256:512]


def _filt_spec(fa, w):
    return pl.pallas_call(
        _filt_spec_kernel,
        grid=(HY_ORDER, FFT_K1 // FFT_TK),
        in_specs=[pl.BlockSpec((2, FFT_TK, FFT_N2, 512), lambda o, i: (0, i, 0, o)),
                  pl.BlockSpec((FFT_TK, 256, 256), lambda o, i: (i, 0, 0))],
        out_specs=pl.BlockSpec((1, 2, FFT_TK, FFT_N2, 256), lambda o, i: (o, 0, i, 0, 0)),
        out_shape=jax.ShapeDtypeStruct((HY_ORDER, 2, FFT_K1, FFT_N2, 256), F32),
        compiler_params=_params("parallel", "parallel"),
        name="hyena_filter_spectrum",
    )(fa, w)


def _fft_mid_kernel(a_ref, w_ref, wt_ref, k_ref, o_ref):
    for j in range(FFT_TK):
        a = jnp.concatenate([a_ref[0, j], a_ref[1, j]], axis=0).astype(BF16)
        x = jnp.dot(w_ref[j], a, preferred_element_type=F32)
        xr, xi = x[:FFT_N2], x[FFT_N2:]
        kr, ki = k_ref[0, 0, j], k_ref[0, 1, j]
        y = jnp.concatenate([xr * kr - xi * ki, xr * ki + xi * kr], axis=0).astype(BF16)
        b = jnp.dot(wt_ref[j], y, preferred_element_type=F32)
        o_ref[0, j] = b[:FFT_N2]
        o_ref[1, j] = b[FFT_N2:]


def _fft_mid(a, w, wt, kf, order):
    blk = pl.BlockSpec((2, FFT_TK, FFT_N2, 256), lambda i: (0, i, 0, 0))
    wblk = pl.BlockSpec((FFT_TK, 256, 256), lambda i: (i, 0, 0))
    return pl.pallas_call(
        _fft_mid_kernel,
        grid=(FFT_K1 // FFT_TK,),
        in_specs=[blk, wblk, wblk,
                  pl.BlockSpec((1, 2, FFT_TK, FFT_N2, 256), lambda i: (order, 0, i, 0, 0))],
        out_specs=blk,
        out_shape=jax.ShapeDtypeStruct((2, FFT_K1, FFT_N2, 256), F32),
        compiler_params=_params("parallel"),
        name="fft_mid",
    )(a, w, wt, kf)


def _fft_inv1_kernel(b_ref, m_ref, x_ref, z_ref, fb_ref, o_ref):
    b2 = b_ref.reshape(2 * FFT_K1 * FFT_TJ, 128)
    x2 = x_ref.reshape(FFT_R * FFT_TJ, 128)
    z2 = z_ref.reshape(FFT_R * FFT_TJ, 128)
    o2 = o_ref.reshape(FFT_R * FFT_TJ, 128)
    for jj in range(FFT_TJ):
        b = jnp.concatenate([b2[pl.ds(jj, FFT_K1, stride=FFT_TJ), :],
                             b2[pl.ds(FFT_K1 * FFT_TJ + jj, FFT_K1, stride=FFT_TJ), :]], axis=0).astype(BF16)
        y = jnp.dot(m_ref[...], b, preferred_element_type=F32)
        rows = pl.ds(jj, FFT_R, stride=FFT_TJ)
        o2[rows, :] = x2[rows, :] * (y + fb_ref[...] * z2[rows, :])


def _fft_inv1(b4, minv, gate3, z3, fb):
    sig = pl.BlockSpec((FFT_R, FFT_TJ, 128), lambda j, cc: (0, j, cc))
    return pl.pallas_call(
        _fft_inv1_kernel,
        grid=(FFT_N2 // FFT_TJ, 2),
        in_specs=[pl.BlockSpec((2, FFT_K1, FFT_TJ, 128), lambda j, cc: (0, 0, j, cc)),
                  _const_spec((FFT_R, FFT_N1)), sig, sig, pl.BlockSpec((1, 128), lambda j, cc: (0, cc))],
        out_specs=sig,
        out_shape=jax.ShapeDtypeStruct((FFT_R, FFT_N2, 256), F32),
        compiler_params=_params("parallel", "parallel"),
        name="fft_inv1_gate",
    )(b4, minv, gate3, z3, fb)


def _ctx_hyena_kernel(a_ref, cw_ref, cb_ref, w1_ref, b1_ref, w2_ref, b2_ref, w3f_ref, w3b_ref, df_ref, db_ref,
                      fb_ref, fc_ref, fi_ref, o_ref):
    zero = jnp.zeros((HALO, 768), F32)
    n = LC + 2 * HALO
    ea = jnp.concatenate([zero, a_ref[...], zero], axis=0)
    w = cw_ref[...]
    y = (cb_ref[...] + w[0:1] * pltpu.roll(ea, 1, 0)[HALO:HALO + LC] + w[1:2] * ea[HALO:HALO + LC]
         + w[2:3] * pltpu.roll(ea, n - 1, 0)[HALO:HALO + LC])
    hf, hb = _filter_block(0, LC, 1.0 / LC, w1_ref[...], b1_ref[...], w2_ref[...], b2_ref[...],
                           w3f_ref[...], w3b_ref[...], df_ref[...], db_ref[...])
    fc = fc_ref[...]
    fi = fi_ref[...]
    sf = _dot3(fc, hf)
    sb = _dot3(fc, hb)
    kr = sf[:LC] + sb[:LC]
    ki = sf[LC:] - sb[LC:]
    z = y[:, 0:256]
    fb = fb_ref[...]
    for o in range(HY_ORDER):
        s = _dot3(fc, z)
        sr, si = s[:LC], s[LC:]
        c0 = 256 * o
        yr = sr * kr[:, c0:c0 + 256] - si * ki[:, c0:c0 + 256]
        yi = sr * ki[:, c0:c0 + 256] + si * kr[:, c0:c0 + 256]
        conv = _dot3(fi, jnp.concatenate([yr, yi], axis=0))
        z = y[:, 256 * (o + 1):256 * (o + 2)] * (conv + fb[o:o + 1] * z)
    o_ref[...] = z


def _ctx_hyena(a, conv_w, conv_b, fp, fbias, fc, fi):
    return pl.pallas_call(
        _ctx_hyena_kernel,
        grid=(1,),
        in_specs=[pl.BlockSpec((LC, 768), lambda i: (L // LC, 0)),
                  _const_spec((3, 768)), _const_spec((1, 768)),
                  _const_spec((128, 128)), _const_spec((1, 128)), _const_spec((128, 128)),
                  _const_spec((1, 128)), _const_spec((2, 128, 512)), _const_spec((2, 128, 512)),
                  _const_spec((1, 512)), _const_spec((1, 512)),
                  _const_spec((HY_ORDER, 256)), _const_spec((2 * LC, LC)), _const_spec((LC, 2 * LC))],
        out_specs=pl.BlockSpec((LC, 256), lambda i: (0, 0)),
        out_shape=jax.ShapeDtypeStruct((LC, 256), F32),
        compiler_params=_params("arbitrary"),
        name="hyena_ctx",
    )(a, conv_w, conv_b, *fp, fbias, fc, fi)


def _s5_in_kernel(ua_ref, ub_ref, w_ref, o_ref):
    u = jnp.concatenate([r[pl.ds(j, S5_RB, stride=S5_T), :].astype(BF16)
                         for j in range(S5_T) for r in (ua_ref, ub_ref)], axis=1)
    o_ref[0] = jnp.dot(u, w_ref[0], preferred_element_type=F32)


def _s5_in(u2, w3):
    return pl.pallas_call(
        _s5_in_kernel,
        grid=(3, S5_ROWS // S5_RB),
        in_specs=[pl.BlockSpec((S5_RB * S5_T, DG // 2), lambda m, r: (r, 0)),
                  pl.BlockSpec((S5_RB * S5_T, DG // 2), lambda m, r: (r, 1)),
                  pl.BlockSpec((1, S5_W, S5_W), lambda m, r: (m, 0, 0))],
        out_specs=pl.BlockSpec((1, S5_RB, S5_W), lambda m, r: (m, r, 0)),
        out_shape=jax.ShapeDtypeStruct((3, S5_ROWS, S5_W), F32),
        compiler_params=_params("parallel", "parallel"),
        name="s5_chunk_in",
    )(u2, u2, w3)


def _s5_scan_kernel(f_ref, b_ref, tab_ref, hf_ref, hb_ref, sf, sb, bf, bb):
    j = pl.program_id(0)

    @pl.when(j == 0)
    def _():
        sf[...] = jnp.zeros_like(sf)
        sb[...] = jnp.zeros_like(sb)

    half = S5_W // 2
    row = lax.broadcasted_iota(jnp.int32, (8, S5_W), 0)

    def cmul(a, x):
        ar, ai = a[:, :half], a[:, half:]
        xr, xi = x[:, :half], x[:, half:]
        return jnp.concatenate([ar * xr - ai * xi, ar * xi + ai * xr], axis=1)

    def shifted(x, k, down):
        if down:
            return jnp.where(row >= k, pltpu.roll(x, k, 0), 0.0)
        return jnp.where(row < 8 - k, pltpu.roll(x, 8 - k, 0), 0.0)

    def tile_scan(x, carry, d, down):
        tab = tab_ref[d]
        y = x
        for n, k in enumerate((1, 2, 4)):
            y = y + cmul(tab[8 + n:9 + n], shifted(y, k, down))
        state = y + cmul(tab[0:8], carry)
        edge = 7 if down else 0
        new_carry = jnp.broadcast_to(state[edge:edge + 1], (8, S5_W))
        before = shifted(state, 1, down)
        first = 0 if down else 7
        return jnp.where(row == first, carry, before), new_carry

    cf = sf[...]
    cb = sb[...]
    for tt in range(S5_SB // 8):
        lo = 8 * tt
        tile, cf = tile_scan(f_ref[0, lo:lo + 8, :], cf, 0, True)
        bf[lo:lo + 8, :] = tile
        lo = S5_SB - 8 - 8 * tt
        tile, cb = tile_scan(b_ref[0, lo:lo + 8, :], cb, 1, False)
        bb[lo:lo + 8, :] = tile
    sf[...] = cf
    sb[...] = cb
    hf_ref[...] = bf[...].astype(BF16)
    hb_ref[...] = bb[...].astype(BF16)


def _s5_scan(fb, tab):
    fidx = lambda j: jnp.where(j == 0, S5_NSB - 1, j - 1)
    bidx = lambda j: S5_NSB - 1 - j
    o = jax.ShapeDtypeStruct((S5_ROWS, S5_W), BF16)
    return pl.pallas_call(
        _s5_scan_kernel,
        grid=(S5_NSB,),
        in_specs=[pl.BlockSpec((1, S5_SB, S5_W), lambda j: (1, fidx(j), 0)),
                  pl.BlockSpec((1, S5_SB, S5_W), lambda j: (2, bidx(j), 0)),
                  _const_spec((2, 16, S5_W))],
        out_specs=[pl.BlockSpec((S5_SB, S5_W), lambda j: (fidx(j), 0)),
                   pl.BlockSpec((S5_SB, S5_W), lambda j: (bidx(j), 0))],
        out_shape=[o, o],
        scratch_shapes=[pltpu.VMEM((8, S5_W), F32), pltpu.VMEM((8, S5_W), F32),
                        pltpu.VMEM((S5_SB, S5_W), F32), pltpu.VMEM((S5_SB, S5_W), F32)],
        compiler_params=_params("arbitrary"),
        name="s5_scan",
    )(fb, fb, tab)


def _s5_out_kernel(y_ref, hf_ref, hb_ref, wf_ref, wb_ref, o_ref, y_sc):
    half = pl.program_id(1)

    @pl.when(half == 0)
    def _():
        y_sc[...] = (y_ref[0] + jnp.dot(hf_ref[...], wf_ref[...], preferred_element_type=F32)
                     + jnp.dot(hb_ref[...], wb_ref[...], preferred_element_type=F32))

    for hh in range(2):
        @pl.when(half == hh)
        def _():
            for j in range(S5_T):
                c0 = DG * j + 128 * hh
                o_ref[pl.ds(j, S5_RB, stride=S5_T), :] = y_sc[:, c0:c0 + 128]


def _s5_out(fb, hf, hb, wf, wb):
    rows = pl.BlockSpec((S5_RB, S5_W), lambda r, c: (r, 0))
    return pl.pallas_call(
        _s5_out_kernel,
        grid=(S5_ROWS // S5_RB, 2),
        in_specs=[pl.BlockSpec((1, S5_RB, S5_W), lambda r, c: (0, r, 0)), rows, rows,
                  _const_spec((S5_W, S5_W)), _const_spec((S5_W, S5_W))],
        out_specs=pl.BlockSpec((S5_RB * S5_T, 128), lambda r, c: (r, c)),
        out_shape=jax.ShapeDtypeStruct((LT, DG), F32),
        scratch_shapes=[pltpu.VMEM((S5_RB, S5_W), F32)],
        compiler_params=_params("parallel", "arbitrary"),
        name="s5_chunk_out",
    )(fb, hf, hb, wf, wb)


def _value_rows(vt):
    return jnp.concatenate([vt, jnp.ones((ATT_VE - HEAD, vt.shape[1]), BF16)], axis=0)


def _attend(k, ve, q_ref, m_sc, acc_sc):
    ks = min(ATT_KS, k.shape[0])
    units = [(c, g) for c in range(k.shape[0] // ks) for g in range(ATT_NG)]

    def scores(unit):
        c, g = unit
        j, c0 = divmod(g * ATT_GQ, ATT_TQ)
        return jnp.dot(k[c * ks:(c + 1) * ks], q_ref[HEAD * j:HEAD * (j + 1), c0:c0 + ATT_GQ],
                       preferred_element_type=F32)

    s_next = scores(units[0])
    for u, (c, g) in enumerate(units):
        s = s_next
        if u + 1 < len(units):
            s_next = scores(units[u + 1])
        m_prev = m_sc[g]
        m_new = jnp.maximum(m_prev, jnp.max(s, axis=0, keepdims=True))
        alpha = jnp.exp2(m_prev - m_new)
        p = jnp.exp2(s - m_new).astype(BF16)
        acc_sc[g] = alpha * acc_sc[g] + jnp.dot(ve[:, c * ks:(c + 1) * ks], p, preferred_element_type=F32)
        m_sc[g] = m_new


def _flash_kernel(q_ref, k_ref, v_ref, kc_ref, vc_ref, o_ref, m_sc, acc_sc):
    ki = pl.program_id(2)

    @pl.when(ki == 0)
    def _():
        m_sc[...] = jnp.full_like(m_sc, -1e30)
        acc_sc[...] = jnp.zeros_like(acc_sc)

    _attend(k_ref[0], _value_rows(v_ref[...]), q_ref, m_sc, acc_sc)

    @pl.when(ki == pl.num_programs(2) - 1)
    def _():
        _attend(kc_ref[0], _value_rows(vc_ref[...]), q_ref, m_sc, acc_sc)
        for g in range(ATT_NG):
            j, c0 = divmod(g * ATT_GQ, ATT_TQ)
            acc = acc_sc[g]
            o_ref[HEAD * j:HEAD * (j + 1), c0:c0 + ATT_GQ] = (acc[:HEAD] / acc[HEAD:HEAD + 1]).astype(BF16)


def _flash(qt, k, vt):
    return pl.pallas_call(
        _flash_kernel,
        grid=(KVH, L // ATT_TQ, L // ATT_TK),
        in_specs=[pl.BlockSpec((2 * HEAD, ATT_TQ), lambda h, qi, ki: (h, qi)),
                  pl.BlockSpec((1, ATT_TK, HEAD), lambda h, qi, ki: (h, ki, 0)),
                  pl.BlockSpec((HEAD, ATT_TK), lambda h, qi, ki: (h, ki)),
                  pl.BlockSpec((1, LC, HEAD), lambda h, qi, ki: (h, L // LC, 0)),
                  pl.BlockSpec((HEAD, LC), lambda h, qi, ki: (h, L // LC))],
        out_specs=pl.BlockSpec((2 * HEAD, ATT_TQ), lambda h, qi, ki: (h, qi)),
        out_shape=jax.ShapeDtypeStruct((QH * HEAD, L), BF16),
        scratch_shapes=[pltpu.VMEM((ATT_NG, 1, ATT_GQ), F32), pltpu.VMEM((ATT_NG, ATT_VE, ATT_GQ), F32)],
        compiler_params=_params("parallel", "parallel", "arbitrary"),
        name="flash_attention",
    )(qt, k, vt, k, vt)


def _ctx_attn_kernel(q_ref, k_ref, v_ref, o_ref):
    ve = _value_rows(v_ref[...])
    for j in range(2):
        s = jnp.dot(k_ref[0], q_ref[HEAD * j:HEAD * (j + 1), :], preferred_element_type=F32)
        p = jnp.exp2(s - jnp.max(s, axis=0, keepdims=True)).astype(BF16)
        acc = jnp.dot(ve, p, preferred_element_type=F32)
        o_ref[HEAD * j:HEAD * (j + 1), :] = (acc[:HEAD] / acc[HEAD:HEAD + 1]).astype(BF16)


def _ctx_attn(qt, k, vt):
    return pl.pallas_call(
        _ctx_attn_kernel,
        grid=(KVH,),
        in_specs=[pl.BlockSpec((2 * HEAD, LC), lambda h: (h, L // LC)),
                  pl.BlockSpec((1, LC, HEAD), lambda h: (h, L // LC, 0)),
                  pl.BlockSpec((HEAD, LC), lambda h: (h, L // LC))],
        out_specs=pl.BlockSpec((2 * HEAD, LC), lambda h: (h, 0)),
        out_shape=jax.ShapeDtypeStruct((QH * HEAD, LC), BF16),
        compiler_params=_params("parallel"),
        name="ctx_attention",
    )(qt, k, vt)


def _gelu_tanh(x):
    return 0.5 * x * (1.0 + jnp.tanh(math.sqrt(2.0 / math.pi) * (x + 0.044715 * (x * x * x))))


def _post_kernel(x_ref, mod_ref, hyl_ref, hyc_ref, s5_ref, po_ref, atl_ref, atc_ref,
                 gw_ref, gb_ref, wo_ref, g1_ref, g2_ref, g3_ref, w1_ref, w2_ref, o_ref):
    is_ctx = pl.program_id(0) >= NLB
    m = mod_ref[0]
    g = _gelu_tanh(s5_ref[...])
    s5o = g * jax.nn.sigmoid(_bdot(g, gw_ref[...]) + gb_ref[...])
    hy = jnp.where(is_ctx, hyc_ref[...], hyl_ref[...])
    at = jnp.where(is_ctx, atc_ref[...], atl_ref[...]).astype(F32).T
    o = (_bdot(hy, wo_ref[0:256, :]) + _bdot(s5o, wo_ref[256:512, :]) + _bdot(po_ref[...], wo_ref[512:768, :])
         + _bdot(at, wo_ref[768:1024, :]))
    x = x_ref[...] + m[2:3] * _rms(o, g1_ref[...])
    h = _rms(x, g2_ref[...]) * (1.0 + m[4:5]) + m[3:4]
    f = jnp.dot(h.astype(BF16), w1_ref[...], preferred_element_type=F32)
    f = jnp.square(jnp.maximum(f, 0.0)).astype(BF16)
    f = jnp.dot(f, w2_ref[...], preferred_element_type=F32)
    o_ref[...] = x + m[5:6] * _rms(f, g3_ref[...])


def _post(x, mods, hy_l, hy_c, s5y, po, at_l, at_c, glu_w, glu_b, w_out, g1, g2, g3, w1, w2, n_blocks):
    row = lambda w: pl.BlockSpec((TR, w), lambda i: (i, 0))
    lat = lambda i: jnp.minimum(i, NLB - 1)
    return pl.pallas_call(
        _post_kernel,
        grid=(n_blocks,),
        in_specs=[row(D),
                  pl.BlockSpec((1, N_MOD, D), lambda i: (i // NLB, 0, 0)),
                  pl.BlockSpec((TR, 256), lambda i: (lat(i), 0)),
                  _const_spec((LC, 256)),
                  row(256), row(256),
                  pl.BlockSpec((QH * HEAD, TR), lambda i: (0, lat(i))),
                  _const_spec((QH * HEAD, LC)),
                  _const_spec((256, 256)), _const_spec((1, 256)), _const_spec((D, D)),
                  _const_spec((1, D)), _const_spec((1, D)), _const_spec((1, D)),
                  _const_spec((D, D_FF)), _const_spec((D_FF, D))],
        out_specs=row(D),
        out_shape=jax.ShapeDtypeStruct((n_blocks * TR, D), F32),
        compiler_params=_params("parallel"),
        name="post_mix_mlp",
    )(x, mods, hy_l, hy_c, s5y, po, at_l, at_c, glu_w, glu_b, w_out, g1, g2, g3, w1, w2)


def _dft_tables():
    n = jnp.arange(128, dtype=jnp.int32)
    m = ((2 * n[:, None] + 1) * n[None, :]) % (2 * FFT_N1)
    ang = m.astype(F32) * (2.0 * math.pi / (2 * FFT_N1))
    c1, s1 = jnp.cos(ang), jnp.sin(ang)
    m1 = jnp.concatenate([c1, -s1], axis=0)
    m1inv = jnp.concatenate([c1.T, -s1.T], axis=1) * (2.0 / FFT_N)
    kk = n[:, None, None] + FFT_N1 * n[None, :, None]
    mm = ((2 * kk + 1) * n[None, None, :]) % (2 * FFT_N)
    phi = mm.astype(F32) * (2.0 * math.pi / (2 * FFT_N))
    cm, sm = jnp.cos(phi), jnp.sin(phi)
    w = jnp.concatenate([jnp.concatenate([cm, sm], axis=2), jnp.concatenate([-sm, cm], axis=2)], axis=1)
    wt = jnp.swapaxes(w, 1, 2)
    nc = jnp.arange(LC, dtype=jnp.int32)
    mc = ((2 * nc[:, None] + 1) * nc[None, :]) % (4 * LC)
    angc = mc.astype(F32) * (2.0 * math.pi / (4 * LC))
    cc, sc = jnp.cos(angc), jnp.sin(angc)
    fc = jnp.concatenate([cc, -sc], axis=0)
    fi = jnp.concatenate([cc.T, -sc.T], axis=1) * (2.0 / (2 * LC))
    return m1.astype(BF16), m1inv.astype(BF16), w.astype(BF16), wt.astype(BF16), fc, fi


def _rope_tables():
    t = jnp.arange(L, dtype=jnp.int32)
    inv = ROPE_THETA ** (-jnp.arange(0, HEAD // 2, 2, dtype=F32) / (HEAD // 2))
    ang = jnp.concatenate([(t // GRID_W).astype(F32)[:, None] * inv[None, :],
                           (t % GRID_W).astype(F32)[:, None] * inv[None, :]], axis=-1)
    ang = jnp.concatenate([ang, jnp.zeros((LC, HEAD // 2), F32)], axis=0)
    c, s = jnp.cos(ang), jnp.sin(ang)
    return jnp.concatenate([c, c, c, c], axis=1), jnp.concatenate([-s, s, -s, s], axis=1)


def _s5_tables(a_re, a_im, log_dt, b_re, b_im, c_re, c_im, d):
    dt = jnp.exp(log_dt)[..., None]
    tau = jnp.arange(S5_T + 1, dtype=F32)[:, None, None, None]
    mag = jnp.exp(a_re * dt * tau)
    pr, pi = mag * jnp.cos(a_im * dt * tau), mag * jnp.sin(a_im * dt * tau)
    lam_re, lam_im = pr[1], pi[1]
    den = a_re * a_re + a_im * a_im
    nr, ni = lam_re - 1.0, lam_im
    cr = (nr * a_re + ni * a_im) / den
    ci = (ni * a_re - nr * a_im) / den
    bb_re = cr[..., None] * b_re - ci[..., None] * b_im
    bb_im = cr[..., None] * b_im + ci[..., None] * b_re
    cl_re = c_re[None] * pr[:, :, :, None, :] - c_im[None] * pi[:, :, :, None, :]
    cl_im = c_re[None] * pi[:, :, :, None, :] + c_im[None] * pr[:, :, :, None, :]
    kk = jnp.einsum('tdghp,dgpk->tdghk', cl_re, bb_re) - jnp.einsum('tdghp,dgpk->tdghk', cl_im, bb_im)
    k0 = kk[0, 0] + kk[0, 1] + d.reshape(S5_G, S5_H)[:, :, None] * jnp.eye(S5_H, dtype=F32)[None]
    kfull = jnp.concatenate([kk[1:S5_T, 1][::-1], k0[None], kk[1:S5_T, 0]], axis=0)
    st = jnp.arange(S5_T)
    kt = kfull[st[None, :] - st[:, None] + S5_T - 1]
    th = S5_T * S5_H
    expand = jnp.einsum('tu,hk->thuk', jnp.eye(S5_T, dtype=F32), jnp.eye(S5_H, dtype=F32))
    expand = jnp.broadcast_to(expand[:, :, :, None, :], (S5_T, S5_H, S5_T, S5_G, S5_H)).reshape(th, S5_W)
    col_g = (jnp.arange(S5_W) // S5_H) % S5_G
    row_g_u = (jnp.arange(S5_W) // S5_H) % S5_G
    row_g_st = (jnp.arange(S5_W) // S5_P) % S5_G
    st_col_g = jnp.arange(S5_G * S5_P) // S5_P

    def widen(compact, row_g):
        full = jnp.dot(compact.astype(BF16), expand.astype(BF16), preferred_element_type=F32)
        return jnp.where(row_g[:, None] == col_g[None, :], full, 0.0).astype(BF16)

    m_intra = widen(kt.transpose(0, 2, 4, 1, 3).reshape(S5_W, th), row_g_u)

    def state_in(pw_re, pw_im, d_):
        re = pw_re[:, :, :, None] * bb_re[d_][None] - pw_im[:, :, :, None] * bb_im[d_][None]
        im = pw_re[:, :, :, None] * bb_im[d_][None] + pw_im[:, :, :, None] * bb_re[d_][None]

        def wide(x):
            c = x.transpose(0, 1, 3, 2).reshape(S5_W, S5_P)
            return jnp.where(row_g_u[:, None] == st_col_g[None, :], jnp.tile(c, (1, S5_G)), 0.0)
        return jnp.concatenate([wide(re), wide(im)], axis=1).astype(BF16)

    m_fst = state_in(pr[:S5_T, 0][::-1], pi[:S5_T, 0][::-1], 0)
    m_bst = state_in(pr[:S5_T, 1], pi[:S5_T, 1], 1)

    def state_out(cre, cim):
        both = jnp.stack([cre, -cim], axis=0)
        return widen(both.transpose(0, 2, 4, 1, 3).reshape(S5_W, th), row_g_st)

    m_fout = state_out(cl_re[1:, 0], cl_im[1:, 0])
    m_bout = state_out(cl_re[1:, 1][::-1], cl_im[1:, 1][::-1])
    mm = (S5_T * jnp.arange(1, 9, dtype=F32))[:, None, None, None]
    mg = jnp.exp(a_re * dt * mm)
    qr = (mg * jnp.cos(a_im * dt * mm)).reshape(8, 2, -1)
    qi = (mg * jnp.sin(a_im * dt * mm)).reshape(8, 2, -1)
    q = jnp.concatenate([qr, qi], axis=2)
    zero = jnp.zeros((5, S5_W), F32)
    steps = jnp.stack([q[0], q[1], q[3]], axis=0)
    lam8 = jnp.stack([jnp.concatenate([q[:, 0], steps[:, 0], zero], axis=0),
                      jnp.concatenate([q[::-1, 1], steps[:, 1], zero], axis=0)], axis=0)
    return jnp.stack([m_intra, m_fst, m_bst], axis=0), lam8, m_fout, m_bout


def _filter_params(w1, b1, w2, b2, w3, decay):
    zf = jnp.zeros((HY_FFN, HY_FFN), F32)
    w1p = jnp.concatenate([w1, jnp.zeros((HY_FFN - w1.shape[0], HY_FFN), F32)], axis=0)
    pair = lambda w: jnp.concatenate([jnp.concatenate([w, zf], axis=1), jnp.concatenate([zf, w], axis=1)], axis=0)
    two = lambda b: jnp.concatenate([b, b]).reshape(1, 2 * HY_FFN)
    w3r = w3.reshape(HY_FFN, HY_ORDER, 2, DG)
    z3 = jnp.zeros((HY_FFN, 512), F32)
    halves = lambda w: jnp.stack([jnp.concatenate([w, z3], axis=0), jnp.concatenate([z3, w], axis=0)], axis=0)
    return (pair(w1p), two(b1), pair(w2), two(b2),
            halves(w3r[:, :, 0].reshape(HY_FFN, 512)), halves(w3r[:, :, 1].reshape(HY_FFN, 512)),
            decay[:, 0].reshape(1, 512), decay[:, 1].reshape(1, 512))


def _block_diag(w):
    g, n, _ = w.shape
    return jnp.einsum('gcd,gj->gcjd', w, jnp.eye(g, dtype=w.dtype)).reshape(g * n, g * n)


def kernel(x, c, ctx, c_ctx, mod_w, mod_b, norm_pre_mix, norm_post_mix, norm_pre_mlp, norm_post_mlp,
           w_in, w_out, hy_conv_w, hy_conv_b, hy_ffn_w1, hy_ffn_b1, hy_ffn_w2, hy_ffn_b2, hy_ffn_w3,
           hy_decay, hy_bias, s5_a_re, s5_a_im, s5_log_dt, s5_b_re, s5_b_im, s5_c_re, s5_c_im, s5_d,
           s5_glu_w, s5_glu_b, pool_w, pool_scale, att_q_norm, att_k_norm, mlp_w1, mlp_w2):
    xs = jnp.concatenate([x[0], ctx[0]], axis=0)
    mods = _modulation(c, c_ctx, mod_w, mod_b)
    m1, m1inv, wk, wkt, fc, fi = _dft_tables()
    cs, sn = _rope_tables()
    perm = jnp.concatenate([jnp.arange(0, HEAD, 2), jnp.arange(1, HEAD, 2)])
    qcols = 1280 + (jnp.arange(QH)[:, None] * HEAD + perm[None, :]).reshape(-1)
    kcols = 1536 + (jnp.arange(KVH)[:, None] * HEAD + perm[None, :]).reshape(-1)
    cols = jnp.concatenate([jnp.arange(1280), qcols, kcols, jnp.arange(1664, IN_COLS)])
    head_mean = _block_diag(jnp.full((QH, HEAD, HEAD), 1.0 / HEAD, F32)).astype(BF16)

    for i in range(DEPTH):
        w_in_i = w_in[i][:, cols].astype(BF16)
        qg = jnp.tile(att_q_norm[i][perm], QH).reshape(1, 256)
        kg = jnp.tile(att_k_norm[i][perm], KVH).reshape(1, 128)
        a, s, p, q, k, v = _premix(xs, mods[i], norm_pre_mix[i].reshape(1, D), w_in_i, cs, sn, qg, kg, head_mean)

        vv, x1, x2, po = _local(a, p, hy_conv_w[i], hy_conv_b[i].reshape(1, 768),
                                _block_diag(pool_w[i]), pool_scale[i].reshape(1, 256))

        fp = _filter_params(hy_ffn_w1[i], hy_ffn_b1[i], hy_ffn_w2[i], hy_ffn_b2[i], hy_ffn_w3[i], hy_decay[i])
        filt = _filter_gen(fp)
        fa = _fft_fwd1(filt.reshape(FFT_R, FFT_N2, 1024), m1)
        kf = _filt_spec(fa, wk)
        time_major = lambda t: t.reshape(LT // FFT_N2, FFT_N2, 256)
        z = time_major(vv)
        gates = (time_major(x1), time_major(x2))
        for o in range(HY_ORDER):
            fa_z = _fft_fwd1(z, m1)
            bm = _fft_mid(fa_z, wk, wkt, kf, o)
            z = _fft_inv1(bm, m1inv, gates[o], z, hy_bias[i][o].reshape(1, 256))
        hy_l = z.reshape(L, 256)
        hy_c = _ctx_hyena(a, hy_conv_w[i], hy_conv_b[i].reshape(1, 768), fp, hy_bias[i], fc, fi)

        w3, lam8, m_fout, m_bout = _s5_tables(s5_a_re[i], s5_a_im[i], s5_log_dt[i], s5_b_re[i], s5_b_im[i],
                                              s5_c_re[i], s5_c_im[i], s5_d[i])
        fb = _s5_in(s, w3)
        hf, hb = _s5_scan(fb, lam8)
        s5y = _s5_out(fb, hf, hb, m_fout, m_bout)

        at_l = _flash(q, k, v)
        at_c = _ctx_attn(q, k, v)

        xs = _post(xs, mods[i], hy_l, hy_c, s5y, po, at_l, at_c,
                   s5_glu_w[i].astype(BF16), s5_glu_b[i].reshape(1, 256), w_out[i].astype(BF16),
                   norm_post_mix[i].reshape(1, D), norm_pre_mlp[i].reshape(1, D), norm_post_mlp[i].reshape(1, D),
                   mlp_w1[i].astype(BF16), mlp_w2[i].astype(BF16),
                   NRB if i < DEPTH - 1 else NLB)
    return xs[None]
```
